```python
import jax, jax.numpy as jnp
from jax import lax
import numpy as np


D_MODEL = 1024
BATCH = 8
SEQ = 4096
DEPTH = 4

CTX_LEN = 256
GRID_W = 64
HEAD_DIM = 64
NA_HEADS = 8
GDN_HEADS = 8
NA_W = NA_HEADS * HEAD_DIM
GDN_W = GDN_HEADS * HEAD_DIM
MIX_W = NA_W + GDN_W
NA_KH = 8
NA_KW = 16
GDN_CHUNK = 64
QKV_CONV = 3
FFN_CONV = 3
D_FF = 2816
ROPE_BASE = 10000.0
ROT_PAIRS_PER_AXIS = HEAD_DIM // 4
RMS_EPS = 1e-6
N_MOD = 6
IN_SPLITS = (NA_W, 2 * NA_W, 3 * NA_W, 3 * NA_W + 3 * GDN_W, 3 * NA_W + 4 * GDN_W,
             3 * NA_W + 4 * GDN_W + 2 * GDN_HEADS)
IN_W = 3 * NA_W + 4 * GDN_W + 4 * GDN_HEADS

kernel_name = 'hybrid_na_gdn_dit_block'


def rms_norm(x, g):
    xf = x.astype(jnp.float32)
    y = xf * lax.rsqrt(jnp.mean(xf * xf, axis=-1, keepdims=True) + RMS_EPS) * g.astype(jnp.float32)
    return y.astype(x.dtype)


def l2_norm(x):
    return x * lax.rsqrt(jnp.sum(x * x, axis=-1, keepdims=True) + RMS_EPS)


def modulate(h, shift, scale):
    return h * (1.0 + scale) + shift


def depthwise_conv_centred(x, w):
    return lax.conv_general_dilated(
        x, w[:, None, :].astype(x.dtype), window_strides=(1,), padding='SAME',
        dimension_numbers=('NWC', 'WIO', 'NWC'), feature_group_count=x.shape[-1])


def axial_rotary_tables(n_tokens):
    t = jnp.arange(n_tokens)
    row = (t // GRID_W).astype(jnp.float32)
    col = (t % GRID_W).astype(jnp.float32)
    inv_freq = ROPE_BASE ** (-jnp.arange(ROT_PAIRS_PER_AXIS, dtype=jnp.float32) / ROT_PAIRS_PER_AXIS)
    ang = jnp.concatenate([row[:, None] * inv_freq, col[:, None] * inv_freq], axis=-1)
    return jnp.cos(ang), jnp.sin(ang)


def apply_rotary(x, cos, sin):
    x1, x2 = jnp.split(x, 2, axis=-1)
    cos = cos[None, :, None, :]
    sin = sin[None, :, None, :]
    return jnp.concatenate([x1 * cos - x2 * sin, x1 * sin + x2 * cos], axis=-1)


def neighbourhood_attention(q, k, v, k_ctx, v_ctx, rpb):
    B, T, H, DH = q.shape
    rows = T // GRID_W
    kh = min(NA_KH, rows)
    kw = NA_KW
    qg = q.reshape(B, rows, GRID_W, H, DH)
    kg = k.reshape(B, rows, GRID_W, H, DH)
    vg = v.reshape(B, rows, GRID_W, H, DH)
    cols = jnp.arange(GRID_W)
    col_start = jnp.clip(cols - kw // 2, 0, GRID_W - kw)
    col_idx = col_start[:, None] + jnp.arange(kw)[None, :]
    dc = col_idx - cols[:, None] + (NA_KW - 1)

    def row_block(r):
        rs = jnp.clip(r - kh // 2, 0, rows - kh)
        k_blk = lax.dynamic_slice_in_dim(kg, rs, kh, axis=1)
        v_blk = lax.dynamic_slice_in_dim(vg, rs, kh, axis=1)
        k_win = k_blk[:, :, col_idx]
        v_win = v_blk[:, :, col_idx]
        q_r = lax.dynamic_index_in_dim(qg, r, axis=1, keepdims=False)
        dr = rs + jnp.arange(kh) - r + (NA_KH - 1)
        bias = rpb[:, dr][:, :, dc]
        s_win = jnp.einsum('bwhd,biwjhd->bwhij', q_r, k_win) + jnp.transpose(bias, (2, 0, 1, 3))[None]
        s_ctx = jnp.einsum('bwhd,blhd->bwhl', q_r, k_ctx)
        s = jnp.concatenate([s_win.reshape(B, GRID_W, H, kh * kw), s_ctx], axis=-1)
        p = jax.nn.softmax(s.astype(jnp.float32), axis=-1).astype(v.dtype)
        p_win = p[..., :kh * kw].reshape(B, GRID_W, H, kh, kw)
        p_ctx = p[..., kh * kw:]
        return (jnp.einsum('bwhij,biwjhd->bwhd', p_win, v_win)
                + jnp.einsum('bwhl,blhd->bwhd', p_ctx, v_ctx))

    o = lax.map(row_block, jnp.arange(rows))
    return jnp.transpose(o, (1, 0, 2, 3, 4)).reshape(B, T, H, DH)


def context_attention(q, k, v):
    s = jnp.einsum('blhd,bmhd->bhlm', q, k)
    p = jax.nn.softmax(s.astype(jnp.float32), axis=-1).astype(v.dtype)
    return jnp.einsum('bhlm,bmhd->blhd', p, v)


def gated_delta_rule_chunked(q, k, v, g, beta, state0):
    B, T, H, DK = q.shape
    DV = v.shape[-1]
    C = GDN_CHUNK
    N = T // C

    def chunks(t):
        t = t.astype(jnp.float32).reshape(B, N, C, H, *t.shape[3:])
        return jnp.moveaxis(t, 3, 2)

    q, k, v, g, beta = (chunks(t) for t in (q, k, v, g, beta))
    g = jnp.cumsum(g, axis=-1)
    idx = jnp.arange(C)
    lower = idx[:, None] >= idx[None, :]
    strict = idx[:, None] > idx[None, :]
    decay = jnp.exp(jnp.where(lower, g[..., :, None] - g[..., None, :], -jnp.inf))
    kb = k * beta[..., None]
    vb = v * beta[..., None]
    m = jnp.where(strict, jnp.einsum('bnhid,bnhjd->bnhij', kb, k) * decay, 0.0)
    a = m + jnp.eye(C, dtype=jnp.float32)
    rhs = jnp.concatenate([vb, kb * jnp.exp(g)[..., None]], axis=-1)
    sol = lax.linalg.triangular_solve(a, rhs, left_side=True, lower=True, unit_diagonal=True)
    u, w = sol[..., :DV], sol[..., DV:]
    qk = jnp.einsum('bnhid,bnhjd->bnhij', q, k) * decay

    def step(S, inp):
        q_c, k_c, u_c, w_c, g_c, qk_c = inp
        v_new = u_c - jnp.einsum('bhck,bhkv->bhcv', w_c, S)
        o = (jnp.einsum('bhck,bhkv->bhcv', q_c * jnp.exp(g_c)[..., None], S)
             + jnp.einsum('bhij,bhjv->bhiv', qk_c, v_new))
        g_last = g_c[..., -1:]
        S = (S * jnp.exp(g_last)[..., None]
             + jnp.einsum('bhck,bhcv->bhkv', k_c * jnp.exp(g_last - g_c)[..., None], v_new))
        return S, o

    xs = tuple(jnp.moveaxis(t, 1, 0) for t in (q, k, u, w, g, qk))
    S, o = lax.scan(step, state0.astype(jnp.float32), xs)
    o = jnp.moveaxis(jnp.moveaxis(o, 0, 1), 2, 3).reshape(B, T, H, DV)
    return o, S


def gdn_qkv(qkv, conv_w, cos=None, sin=None):
    qkv = jax.nn.silu(depthwise_conv_centred(qkv, conv_w)).astype(jnp.float32)
    q, k, v = jnp.split(qkv, 3, axis=-1)
    heads = lambda t: t.reshape(*t.shape[:-1], GDN_HEADS, HEAD_DIM)
    q, k, v = l2_norm(heads(q)), l2_norm(heads(k)), heads(v)
    if cos is not None:
        q, k = apply_rotary(q, cos, sin), apply_rotary(k, cos, sin)
    return q * HEAD_DIM ** -0.5, k, v


def gdn_gates(beta_logits, decay_in, a_log, dt_bias, d):
    sl = slice(d * GDN_HEADS, (d + 1) * GDN_HEADS)
    beta = jax.nn.sigmoid(beta_logits[..., sl].astype(jnp.float32))
    g = -jnp.exp(a_log[d]) * jax.nn.softplus(decay_in[..., sl].astype(jnp.float32) + dt_bias[d])
    return g, beta


def hybrid_mixer(h_lat, h_ctx, w_in, qkv_conv, a_log, dt_bias, gdn_norm, rpb, w_out, cos, sin):
    B = h_lat.shape[0]
    na_q_l, na_k_l, na_v_l, gqkv_l, z_l, beta_l, dec_l = jnp.split(h_lat @ w_in, IN_SPLITS, axis=-1)
    na_q_c, na_k_c, na_v_c, gqkv_c, z_c, beta_c, dec_c = jnp.split(h_ctx @ w_in, IN_SPLITS, axis=-1)
    na_heads = lambda t: t.reshape(*t.shape[:-1], NA_HEADS, HEAD_DIM)
    scale = HEAD_DIM ** -0.5

    k_c, v_c = na_heads(na_k_c), na_heads(na_v_c)
    o_na_lat = neighbourhood_attention(na_heads(na_q_l) * scale, na_heads(na_k_l), na_heads(na_v_l),
                                       k_c, v_c, rpb)
    o_na_ctx = context_attention(na_heads(na_q_c) * scale, k_c, v_c)

    q_l, k_l, v_l = gdn_qkv(gqkv_l, qkv_conv, cos, sin)
    q_c, k_c2, v_c2 = gdn_qkv(gqkv_c, qkv_conv)
    s0 = jnp.zeros((B, GDN_HEADS, HEAD_DIM, HEAD_DIM), jnp.float32)
    flip = lambda t: jnp.flip(t, axis=1)
    g_lf, b_lf = gdn_gates(beta_l, dec_l, a_log, dt_bias, 0)
    g_cf, b_cf = gdn_gates(beta_c, dec_c, a_log, dt_bias, 0)
    g_lb, b_lb = gdn_gates(beta_l, dec_l, a_log, dt_bias, 1)
    g_cb, b_cb = gdn_gates(beta_c, dec_c, a_log, dt_bias, 1)
    o_cf, s_cf = gated_delta_rule_chunked(q_c, k_c2, v_c2, g_cf, b_cf, s0)
    o_lf, _ = gated_delta_rule_chunked(q_l, k_l, v_l, g_lf, b_lf, s_cf)
    o_cb, s_cb = gated_delta_rule_chunked(flip(q_c), flip(k_c2), flip(v_c2), flip(g_cb), flip(b_cb), s0)
    o_lb, _ = gated_delta_rule_chunked(flip(q_l), flip(k_l), flip(v_l), flip(g_lb), flip(b_lb), s_cb)
    gdn_heads = lambda t: t.reshape(*t.shape[:-1], GDN_HEADS, HEAD_DIM)
    o_gdn_lat = rms_norm(o_lf + flip(o_lb), gdn_norm) * jax.nn.silu(gdn_heads(z_l).astype(jnp.float32))
    o_gdn_ctx = rms_norm(o_cf + flip(o_cb), gdn_norm) * jax.nn.silu(gdn_heads(z_c).astype(jnp.float32))

    def merge(o_na, o_gdn, ref):
        o = jnp.concatenate([o_na.reshape(*o_na.shape[:2], NA_W),
                             o_gdn.reshape(*o_gdn.shape[:2], GDN_W).astype(o_na.dtype)], axis=-1)
        return (o @ w_out).astype(ref.dtype)

    return merge(o_na_lat, o_gdn_lat, h_lat), merge(o_na_ctx, o_gdn_ctx, h_ctx)


def conv_ffn(h, w_up, conv_w, w_down):
    u = depthwise_conv_centred(h @ w_up, conv_w)
    gate, val = jnp.split(u, 2, axis=-1)
    return (jax.nn.silu(gate) * val) @ w_down


def setup_inputs(seed: int = 0) -> dict:
    key = jax.random.key(seed)
    ks = jax.random.split(key, 24)
    f32 = jnp.float32
    nrm = lambda k, shape, s: jax.random.normal(k, shape, f32) * s
    gain = lambda k, shape: 1.0 + 0.1 * jax.random.normal(k, shape, f32)
    dt = jax.random.uniform(ks[11], (DEPTH, 2, GDN_HEADS), f32, 0.001, 0.1)
    return {
        'x': nrm(ks[0], (BATCH, SEQ, D_MODEL), 1.0),
        'c': nrm(ks[1], (BATCH, D_MODEL), 1.0),
        'ctx': nrm(ks[2], (BATCH, CTX_LEN, D_MODEL), 1.0),
        'c_ctx': nrm(ks[3], (D_MODEL,), 1.0),
        'ada_w': nrm(ks[4], (DEPTH, D_MODEL, N_MOD * D_MODEL), D_MODEL ** -0.5),
        'ada_b': nrm(ks[5], (DEPTH, N_MOD * D_MODEL), 0.01),
        'norm_mix_pre': gain(ks[6], (DEPTH, D_MODEL)),
        'norm_mix_post': gain(ks[7], (DEPTH, D_MODEL)),
        'w_in': nrm(ks[8], (DEPTH, D_MODEL, IN_W), D_MODEL ** -0.5),
        'qkv_conv': nrm(ks[9], (DEPTH, QKV_CONV, 3 * GDN_W), QKV_CONV ** -0.5),
        'a_log': jnp.log(jax.random.uniform(ks[10], (DEPTH, 2, GDN_HEADS), f32, 1.0, 16.0)),
        'dt_bias': dt + jnp.log(-jnp.expm1(-dt)),
        'gdn_norm': gain(ks[12], (DEPTH, HEAD_DIM)),
        'rpb': nrm(ks[13], (DEPTH, NA_HEADS, 2 * NA_KH - 1, 2 * NA_KW - 1), 0.5),
        'w_out': nrm(ks[14], (DEPTH, MIX_W, D_MODEL), MIX_W ** -0.5),
        'norm_ffn_pre': gain(ks[15], (DEPTH, D_MODEL)),
        'norm_ffn_post': gain(ks[16], (DEPTH, D_MODEL)),
        'ffn_up': nrm(ks[17], (DEPTH, D_MODEL, 2 * D_FF), D_MODEL ** -0.5),
        'ffn_conv': nrm(ks[18], (DEPTH, FFN_CONV, 2 * D_FF), FFN_CONV ** -0.5),
        'ffn_down': nrm(ks[19], (DEPTH, D_FF, D_MODEL), D_FF ** -0.5),
    }


def reference(x, c, ctx, c_ctx, ada_w, ada_b, norm_mix_pre, norm_mix_post, w_in, qkv_conv, a_log,
              dt_bias, gdn_norm, rpb, w_out, norm_ffn_pre, norm_ffn_post, ffn_up, ffn_conv, ffn_down):
    cos, sin = axial_rotary_tables(x.shape[1])
    xc = ctx
    for l in range(DEPTH):
        last = l == DEPTH - 1
        m_lat = [m[:, None, :] for m in jnp.split(jax.nn.silu(c) @ ada_w[l] + ada_b[l], N_MOD, axis=-1)]
        m_ctx = jnp.split(jax.nn.silu(c_ctx) @ ada_w[l] + ada_b[l], N_MOD, axis=-1)

        h_lat = modulate(rms_norm(x, norm_mix_pre[l]), m_lat[0], m_lat[1])
        h_ctx = modulate(rms_norm(xc, norm_mix_pre[l]), m_ctx[0], m_ctx[1])
        y_lat, y_ctx = hybrid_mixer(h_lat, h_ctx, w_in[l], qkv_conv[l], a_log[l], dt_bias[l], gdn_norm[l],
                                    rpb[l], w_out[l], cos, sin)
        x = x + m_lat[2] * rms_norm(y_lat, norm_mix_post[l])
        if not last:
            xc = xc + m_ctx[2] * rms_norm(y_ctx, norm_mix_post[l])

        f_lat = conv_ffn(modulate(rms_norm(x, norm_ffn_pre[l]), m_lat[3], m_lat[4]),
                         ffn_up[l], ffn_conv[l], ffn_down[l])
        x = x + m_lat[5] * rms_norm(f_lat, norm_ffn_post[l])
        if not last:
            f_ctx = conv_ffn(modulate(rms_norm(xc, norm_ffn_pre[l]), m_ctx[3], m_ctx[4]),
                             ffn_up[l], ffn_conv[l], ffn_down[l])
            xc = xc + m_ctx[5] * rms_norm(f_ctx, norm_ffn_post[l])
    return x
```

```python
import functools

import jax
import jax.numpy as jnp
import numpy as np
from jax import lax
from jax.experimental import pallas as pl
from jax.experimental.pallas import tpu as pltpu

F32 = jnp.float32
BF16 = jnp.bfloat16

HEAD_DIM = 64
NA_HEADS = 8
GDN_HEADS = 8
NA_W = NA_HEADS * HEAD_DIM
GDN_W = GDN_HEADS * HEAD_DIM
GRID_W = 64
NA_KH = 8
NA_KW = 16
CHUNK = 64
ROPE_BASE = 10000.0
RMS_EPS = 1e-6
N_MOD = 6
TILE = 256
CHUNKS_PER_TILE = TILE // CHUNK
HALO = 8
GROUP_W = 256
HEADS_PER_GROUP = GROUP_W // HEAD_DIM
N_LEVELS = 6
MASK_NEG = -1e30
GATE_PAD = 128
FFN_COLS = 256
VMEM_LIMIT = 56 * 1024 * 1024


def _cparams(n_axes):
    return pltpu.CompilerParams(dimension_semantics=("arbitrary",) * n_axes,
                                vmem_limit_bytes=VMEM_LIMIT)


def _const_spec(shape):
    nd = len(shape)
    return pl.BlockSpec(shape, lambda *_: (0,) * nd)


def _sigmoid(x):
    return 1.0 / (1.0 + jnp.exp(-x))


def _silu(x):
    return x * _sigmoid(x)


def _softplus(x):
    return jnp.maximum(x, 0.0) + jnp.log(1.0 + jnp.exp(-jnp.abs(x)))


def _rms(x, gain):
    ms = jnp.mean(x * x, axis=-1, keepdims=True)
    return x * lax.rsqrt(ms + RMS_EPS) * gain


def _dot(a, b):
    return jnp.dot(a, b, preferred_element_type=F32)


def _dot_nt(a, b):
    return lax.dot_general(a, b, (((1,), (1,)), ((), ())), preferred_element_type=F32)


def _mod_row(b, s, n_batch):
    return jnp.where(s == 0, n_batch, b)


def _modulation_kernel(c_ref, w_ref, b_ref, o_ref):
    a = _silu(c_ref[...]).astype(BF16)
    o_ref[0] = _dot(a, w_ref[0].astype(BF16)) + b_ref[0]


def _modulation(cc, ada_w, ada_b):
    depth, d, n = ada_w.shape
    rows = cc.shape[0]
    nblk = n // N_MOD
    return pl.pallas_call(
        _modulation_kernel,
        grid=(depth, N_MOD),
        in_specs=[pl.BlockSpec((rows, d), lambda l, j: (0, 0)),
                  pl.BlockSpec((1, d, nblk), lambda l, j: (l, 0, j)),
                  pl.BlockSpec((1, 1, nblk), lambda l, j: (l, 0, j))],
        out_specs=pl.BlockSpec((1, rows, nblk), lambda l, j: (l, 0, j)),
        out_shape=jax.ShapeDtypeStruct((depth, rows, n), F32),
        compiler_params=_cparams(2),
        name="modulation",
    )(cc, ada_w, ada_b.reshape(depth, 1, n))


def _pre_mixer_kernel(x_ref, mod_ref, g_ref, w_ref, q_ref, k_ref, v_ref, gq_ref, z_ref, gt_ref):
    d = x_ref.shape[-1]
    m = mod_ref[0]
    h = _rms(x_ref[0], g_ref[...]) * (1.0 + m[:, d:2 * d]) + m[:, 0:d]
    hb = h.astype(BF16)
    o = 0
    q_ref[0] = (_dot(hb, w_ref[:, o:o + NA_W]) * HEAD_DIM ** -0.5).astype(BF16)
    o += NA_W
    k_ref[0] = _dot(hb, w_ref[:, o:o + NA_W]).astype(BF16)
    o += NA_W
    v_ref[0] = _dot(hb, w_ref[:, o:o + NA_W]).astype(BF16)
    o += NA_W
    gq_ref[0] = _dot(hb, w_ref[:, o:o + 3 * GDN_W]).astype(BF16)
    o += 3 * GDN_W
    z_ref[0] = _dot(hb, w_ref[:, o:o + GDN_W]).astype(BF16)
    o += GDN_W
    gt_ref[0] = _dot(hb, w_ref[:, o:o + GATE_PAD])


def _pre_mixer(x_all, mod_l, gain, w_in_p):
    nb, s_len, d = x_all.shape
    ns = s_len // TILE
    tok = lambda w: pl.BlockSpec((1, TILE, w), lambda b, s: (b, s, 0))
    shp = lambda w, dt: jax.ShapeDtypeStruct((nb, s_len, w), dt)
    return pl.pallas_call(
        _pre_mixer_kernel,
        grid=(nb, ns),
        in_specs=[tok(d),
                  pl.BlockSpec((1, 1, N_MOD * d), lambda b, s: (_mod_row(b, s, nb), 0, 0)),
                  _const_spec((1, d)),
                  _const_spec(w_in_p.shape)],
        out_specs=[tok(NA_W), tok(NA_W), tok(NA_W), tok(3 * GDN_W), tok(GDN_W), tok(GATE_PAD)],
        out_shape=[shp(NA_W, BF16), shp(NA_W, BF16), shp(NA_W, BF16), shp(3 * GDN_W, BF16),
                   shp(GDN_W, BF16), shp(GATE_PAD, F32)],
        compiler_params=_cparams(2),
        name="pre_mixer",
    )(x_all, mod_l, gain, w_in_p)


def _block_diag(x, bdmask):
    return jnp.concatenate([x] * HEADS_PER_GROUP, axis=0) * bdmask


def _split3(x):
    a = x.astype(BF16)
    r = x - a.astype(F32)
    b = r.astype(BF16)
    c = (r - b.astype(F32)).astype(BF16)
    return a, b, c


def _expand_heads(a, c0):
    rows = a.shape[0]
    lane = lax.broadcasted_iota(jnp.int32, (rows, 128), 1)
    parts = []
    for p in range(GDN_HEADS // 2):
        lo = jnp.broadcast_to(a[:, c0 + 2 * p:c0 + 2 * p + 1], (rows, 128))
        hi = jnp.broadcast_to(a[:, c0 + 2 * p + 1:c0 + 2 * p + 2], (rows, 128))
        parts.append(jnp.where(lane < HEAD_DIM, lo, hi))
    return jnp.concatenate(parts, axis=1)


def _gdn_local_kernel(x_ref, xp_ref, xn_ref, gt_ref, cw_ref, av_ref, dt_ref, cos_ref, sin_ref,
                      ones_ref, bd_ref, lvl_ref,
                      u_ref, w_ref, qg_ref, qk_ref, kdt_ref, eg_ref,
                      q_s, k_s, v_s, gc_s, be_s):
    s = pl.program_id(1)
    ns = pl.num_programs(1)
    width = x_ref.shape[-1]

    x = x_ref[0].astype(F32)
    prev_ok = (s >= 2).astype(F32)
    next_ok = jnp.logical_and(s >= 1, s < ns - 1).astype(F32)
    prev_row = xp_ref[0, HALO - 1:HALO, :].astype(F32) * prev_ok
    next_row = xn_ref[0, 0:1, :].astype(F32) * next_ok
    ri = lax.broadcasted_iota(jnp.int32, (TILE, width), 0)
    x_m1 = jnp.where(ri == 0, prev_row, pltpu.roll(x, 1, 0))
    x_p1 = jnp.where(ri == TILE - 1, next_row, pltpu.roll(x, TILE - 1, 0))
    y = cw_ref[0:1, :] * x_m1 + cw_ref[1:2, :] * x + cw_ref[2:3, :] * x_p1
    a = _silu(y)
    q = a[:, 0:GDN_W]
    k = a[:, GDN_W:2 * GDN_W]
    v_s[...] = a[:, 2 * GDN_W:3 * GDN_W]

    ones_bd = ones_ref[...]

    def seg_sum(t):
        hi = t.astype(BF16)
        lo = (t - hi.astype(F32)).astype(BF16)
        return _dot(hi, ones_bd) + _dot(lo, ones_bd)

    q = q * lax.rsqrt(seg_sum(q * q) + RMS_EPS)
    k = k * lax.rsqrt(seg_sum(k * k) + RMS_EPS)
    cos_t = cos_ref[...]
    sin_t = sin_ref[...]
    lane = lax.broadcasted_iota(jnp.int32, (TILE, 128), 1)
    first_half = (lane % HEAD_DIM) < HEAD_DIM // 2

    def rope(t):
        parts = []
        for p in range(GDN_W // 128):
            ts = t[:, 128 * p:128 * (p + 1)]
            partner = jnp.where(first_half, pltpu.roll(ts, 128 - HEAD_DIM // 2, 1),
                                pltpu.roll(ts, HEAD_DIM // 2, 1))
            parts.append(ts * cos_t + partner * sin_t)
        return jnp.concatenate(parts, axis=1)

    q_s[...] = rope(q) * HEAD_DIM ** -0.5
    k_s[...] = rope(k)

    gt = gt_ref[0]
    beta = _sigmoid(gt)
    g_raw = -jnp.exp(av_ref[...]) * _softplus(gt + dt_ref[...])
    r_i = lax.broadcasted_iota(jnp.int32, (TILE, TILE), 0)
    c_i = lax.broadcasted_iota(jnp.int32, (TILE, TILE), 1)
    same_chunk = (r_i // CHUNK) == (c_i // CHUNK)
    tri_f = jnp.where(jnp.logical_and(same_chunk, r_i >= c_i), 1.0, 0.0).astype(BF16)
    tri_b = jnp.where(jnp.logical_and(same_chunk, r_i <= c_i), 1.0, 0.0).astype(BF16)
    g1, g2, g3 = _split3(g_raw)
    cum_f = _dot(tri_f, g1) + _dot(tri_f, g2) + _dot(tri_f, g3)
    cum_b = _dot(tri_b, g1) + _dot(tri_b, g2) + _dot(tri_b, g3)
    gc_s[0] = _expand_heads(cum_f, 2 * GDN_HEADS)
    gc_s[1] = _expand_heads(cum_b, 3 * GDN_HEADS)
    be_s[0] = _expand_heads(beta, 0)
    be_s[1] = _expand_heads(beta, GDN_HEADS)

    bdmask = bd_ref[...]
    eye = lvl_ref[N_LEVELS]
    ii = lax.broadcasted_iota(jnp.int32, (CHUNK, GROUP_W), 0)
    jj = lax.broadcasted_iota(jnp.int32, (CHUNK, GROUP_W), 1) % HEAD_DIM

    def sdot(a_stack, b_stack):
        return _dot(a_stack.astype(BF16), _block_diag(b_stack.astype(BF16), bdmask))

    def problem(c, d, gi):
        rows = pl.ds(pl.multiple_of(c * CHUNK, CHUNK), CHUNK)
        lanes = slice(gi * GROUP_W, (gi + 1) * GROUP_W)
        kk = k_s[rows, lanes]
        qq = q_s[rows, lanes]
        vv = v_s[rows, lanes]
        gc = gc_s[d, rows, lanes]
        be = be_s[d, rows, lanes]
        tri = (ii >= jj) if d == 0 else (ii <= jj)
        g_last = gc[CHUNK - 1:CHUNK, :] if d == 0 else gc[0:1, :]
        kb = kk * be
        vb = vv * be
        e_gc = jnp.exp(gc)
        kbg = kb * e_gc
        qg = qq * e_gc
        kd = kk * jnp.exp(g_last - gc)
        gc_t = jnp.sum(gc * eye, axis=0, keepdims=True)
        dec = jnp.exp(jnp.where(tri, gc - gc_t, MASK_NEG))
        k_bd = _block_diag(kk.astype(BF16), bdmask)
        gram = _dot_nt(jnp.concatenate([kb, qq], axis=0).astype(BF16), k_bd)
        m = gram[0:CHUNK] * dec * (1.0 - eye)
        qk = gram[CHUNK:2 * CHUNK] * dec
        xinv = eye - m * lvl_ref[0]
        for lv in range(1, N_LEVELS):
            l_n = m * lvl_ref[lv]
            xinv = xinv - sdot(sdot(xinv, l_n), xinv)
        u_ref[d, 0, rows, lanes] = sdot(xinv, vb).astype(BF16)
        w_ref[d, 0, rows, lanes] = sdot(xinv, kbg).astype(BF16)
        qg_ref[d, 0, rows, lanes] = qg.astype(BF16)
        qk_ref[d, 0, rows, lanes] = qk.astype(BF16)
        kdt_ref[d, 0, rows, lanes] = _dot_nt(
            eye.astype(BF16), _block_diag(kd.astype(BF16), bdmask)).astype(BF16)
        return jnp.exp(g_last)

    def chunk_body(c, carry):
        for d in range(2):
            eg = jnp.concatenate([problem(c, d, gi) for gi in range(GDN_W // GROUP_W)], axis=1)
            eg_ref[d, 0, 0, pl.ds(c, 1), :] = eg
        return carry

    lax.fori_loop(0, CHUNKS_PER_TILE, chunk_body, 0)


def _gdn_local(gqkv, gates, conv_w, avec, dtvec, cos_t, sin_t, ones_bd, bdmask, lvl):
    nb, s_len, width = gqkv.shape
    ns = s_len // TILE
    hpt = TILE // HALO
    nh = s_len // HALO
    dir_spec = pl.BlockSpec((2, 1, TILE, GDN_W), lambda b, s: (0, b, s, 0))
    dir_shape = jax.ShapeDtypeStruct((2, nb, s_len, GDN_W), BF16)
    return pl.pallas_call(
        _gdn_local_kernel,
        grid=(nb, ns),
        in_specs=[pl.BlockSpec((1, TILE, width), lambda b, s: (b, s, 0)),
                  pl.BlockSpec((1, HALO, width), lambda b, s: (b, jnp.maximum(s * hpt - 1, 0), 0)),
                  pl.BlockSpec((1, HALO, width),
                               lambda b, s: (b, jnp.minimum((s + 1) * hpt, nh - 1), 0)),
                  pl.BlockSpec((1, TILE, GATE_PAD), lambda b, s: (b, s, 0)),
                  _const_spec(conv_w.shape),
                  _const_spec(avec.shape),
                  _const_spec(dtvec.shape),
                  pl.BlockSpec((TILE, 128), lambda b, s: (s, 0)),
                  pl.BlockSpec((TILE, 128), lambda b, s: (s, 0)),
                  _const_spec(ones_bd.shape),
                  _const_spec(bdmask.shape),
                  _const_spec(lvl.shape)],
        out_specs=[dir_spec] * 5 + [pl.BlockSpec((2, 1, 1, CHUNKS_PER_TILE, GDN_W),
                                                 lambda b, s: (0, b, s, 0, 0))],
        out_shape=[dir_shape] * 5 + [jax.ShapeDtypeStruct((2, nb, ns, CHUNKS_PER_TILE, GDN_W), F32)],
        scratch_shapes=[pltpu.VMEM((TILE, GDN_W), F32), pltpu.VMEM((TILE, GDN_W), F32),
                        pltpu.VMEM((TILE, GDN_W), F32), pltpu.VMEM((2, TILE, GDN_W), F32),
                        pltpu.VMEM((2, TILE, GDN_W), F32)],
        compiler_params=_cparams(2),
        name="gdn_local",
    )(gqkv, gqkv, gqkv, gates, conv_w, avec, dtvec, cos_t, sin_t, ones_bd, bdmask, lvl)


def _gdn_scan_kernel(uf, wf, qgf, qkf, kdf, egf, ub, wb, qgb, qkb, kdb, egb, bd_ref,
                     of_ref, ob_ref, state):
    j = pl.program_id(1)

    @pl.when(j == 0)
    def _():
        state[...] = jnp.zeros_like(state)

    bdmask = bd_ref[...]
    ins = ((uf, wf, qgf, qkf, kdf, egf, of_ref), (ub, wb, qgb, qkb, kdb, egb, ob_ref))
    for ci in range(CHUNKS_PER_TILE):
        for d in range(2):
            u_r, w_r, qg_r, qk_r, kd_r, eg_r, o_r = ins[d]
            c = ci if d == 0 else CHUNKS_PER_TILE - 1 - ci
            rows = slice(c * CHUNK, (c + 1) * CHUNK)
            for gi in range(GDN_W // GROUP_W):
                lanes = slice(gi * GROUP_W, (gi + 1) * GROUP_W)
                st = state[2 * d + gi]
                lhs1 = jnp.concatenate([w_r[0, 0, rows, lanes], qg_r[0, 0, rows, lanes]], axis=0)
                r1 = _dot(lhs1, _block_diag(st.astype(BF16), bdmask))
                v_new = u_r[0, 0, rows, lanes].astype(F32) - r1[0:CHUNK]
                lhs2 = jnp.concatenate([qk_r[0, 0, rows, lanes], kd_r[0, 0, rows, lanes]], axis=0)
                r2 = _dot(lhs2, _block_diag(v_new.astype(BF16), bdmask))
                o_r[0, rows, lanes] = (r1[CHUNK:2 * CHUNK] + r2[0:CHUNK]).astype(BF16)
                state[2 * d + gi] = st * eg_r[0, 0, 0, c:c + 1, lanes] + r2[CHUNK:2 * CHUNK]


def _gdn_scan(u, w, qg, qk, kdt, eg, bdmask):
    _, nb, s_len, _ = u.shape
    ns = s_len // TILE
    bwd = lambda j: jnp.where(j == 0, 0, ns - j)
    f_spec = pl.BlockSpec((1, 1, TILE, GDN_W), lambda b, j: (0, b, j, 0))
    b_spec = pl.BlockSpec((1, 1, TILE, GDN_W), lambda b, j: (1, b, bwd(j), 0))
    egf_spec = pl.BlockSpec((1, 1, 1, CHUNKS_PER_TILE, GDN_W), lambda b, j: (0, b, j, 0, 0))
    egb_spec = pl.BlockSpec((1, 1, 1, CHUNKS_PER_TILE, GDN_W), lambda b, j: (1, b, bwd(j), 0, 0))
    out_shape = jax.ShapeDtypeStruct((nb, s_len, GDN_W), BF16)
    return pl.pallas_call(
        _gdn_scan_kernel,
        grid=(nb, ns),
        in_specs=[f_spec] * 5 + [egf_spec] + [b_spec] * 5 + [egb_spec] + [_const_spec(bdmask.shape)],
        out_specs=[pl.BlockSpec((1, TILE, GDN_W), lambda b, j: (b, j, 0)),
                   pl.BlockSpec((1, TILE, GDN_W), lambda b, j: (b, bwd(j), 0))],
        out_shape=[out_shape, out_shape],
        scratch_shapes=[pltpu.VMEM((2 * GDN_W // GROUP_W, CHUNK, GROUP_W), F32)],
        compiler_params=_cparams(2),
        name="gdn_scan",
    )(u, w, qg, qk, kdt, eg, u, w, qg, qk, kdt, eg, bdmask)


def _na_kernel(q_ref, k_ref, v_ref, bias_ref, o_ref, *, n_rows):
    s = pl.program_id(2)
    lane = lax.broadcasted_iota(jnp.int32, (1, 128), 1)
    head_mask = (lane < HEAD_DIM, lane >= HEAD_DIM)

    @pl.when(s == 0)
    def _():
        q = q_ref[0]
        kc = k_ref[0, 0:TILE, :]
        vc = v_ref[0, 0:TILE, :]
        outs = []
        for h in range(2):
            qh = jnp.where(head_mask[h], q, jnp.zeros_like(q))
            sc = _dot_nt(qh, kc)
            p = jnp.exp(sc - jnp.max(sc, axis=-1, keepdims=True))
            outs.append(_dot(p.astype(BF16), vc) / jnp.sum(p, axis=-1, keepdims=True))
        o_ref[0] = jnp.where(head_mask[0], outs[0], outs[1]).astype(BF16)

    @pl.when(s > 0)
    def _():
        kc = k_ref[0, 0:TILE, :]
        vc = v_ref[0, 0:TILE, :]
        rows_per_tile = TILE // GRID_W
        for r in range(rows_per_tile):
            row = (s - 1) * rows_per_tile + r
            rs = jnp.clip(row - NA_KH // 2, 0, n_rows - NA_KH)
            cls = rs - row + (NA_KH - 1)
            start = pl.multiple_of(TILE + rs * GRID_W, GRID_W)
            kw = k_ref[0, pl.ds(start, NA_KH * GRID_W), :]
            vw = v_ref[0, pl.ds(start, NA_KH * GRID_W), :]
            q = q_ref[0, r * GRID_W:(r + 1) * GRID_W, :]
            outs = []
            for h in range(2):
                qh = jnp.where(head_mask[h], q, jnp.zeros_like(q))
                sw = _dot_nt(qh, kw) + bias_ref[h, cls]
                sc = _dot_nt(qh, kc)
                mx = jnp.maximum(jnp.max(sw, axis=-1, keepdims=True),
                                 jnp.max(sc, axis=-1, keepdims=True))
                pw = jnp.exp(sw - mx)
                pc = jnp.exp(sc - mx)
                den = jnp.sum(pw, axis=-1, keepdims=True) + jnp.sum(pc, axis=-1, keepdims=True)
                outs.append((_dot(pw.astype(BF16), vw) + _dot(pc.astype(BF16), vc)) / den)
            o_ref[0, r * GRID_W:(r + 1) * GRID_W, :] = jnp.where(
                head_mask[0], outs[0], outs[1]).astype(BF16)


def _na_attention(q, k, v, bias_tab):
    nb, s_len, _ = q.shape
    ns = s_len // TILE
    n_rows = (s_len - TILE) // GRID_W
    return pl.pallas_call(
        functools.partial(_na_kernel, n_rows=n_rows),
        grid=(nb, NA_W // 128, ns),
        in_specs=[pl.BlockSpec((1, TILE, 128), lambda b, hp, s: (b, s, hp)),
                  pl.BlockSpec((1, s_len, 128), lambda b, hp, s: (b, 0, hp)),
                  pl.BlockSpec((1, s_len, 128), lambda b, hp, s: (b, 0, hp)),
                  pl.BlockSpec((2, NA_KH, GRID_W, NA_KH * GRID_W), lambda b, hp, s: (hp, 0, 0, 0))],
        out_specs=pl.BlockSpec((1, TILE, 128), lambda b, hp, s: (b, s, hp)),
        out_shape=jax.ShapeDtypeStruct((nb, s_len, NA_W), BF16),
        compiler_params=_cparams(3),
        name="na_attn",
    )(q, k, v, bias_tab)


def _na_bias_table(rpb):
    qi = jnp.arange(GRID_W)[:, None]
    kc = jnp.arange(GRID_W)[None, :]
    cs = jnp.clip(qi - NA_KW // 2, 0, GRID_W - NA_KW)
    in_win = jnp.logical_and(kc >= cs, kc < cs + NA_KW)
    dc = jnp.clip(kc - qi + (NA_KW - 1), 0, 2 * NA_KW - 2)
    dr = jnp.arange(NA_KH)[:, None] + jnp.arange(NA_KH)[None, :]
    tab = rpb[:, dr][:, :, :, dc]
    tab = jnp.where(in_win[None, None, None], tab, MASK_NEG)
    tab = jnp.transpose(tab, (0, 1, 3, 2, 4))
    return tab.reshape(rpb.shape[0], NA_KH, GRID_W, NA_KH * GRID_W).astype(F32)


def _post_mixer_kernel(x_ref, mod_ref, ona_ref, of_ref, ob_ref, z_ref, gg_ref, gp_ref, w_ref,
                       ones_ref, o_ref):
    d = x_ref.shape[-1]
    og = of_ref[0].astype(F32) + ob_ref[0].astype(F32)
    sq = og * og
    hi = sq.astype(BF16)
    lo = (sq - hi.astype(F32)).astype(BF16)
    ms = (_dot(hi, ones_ref[...]) + _dot(lo, ones_ref[...])) * (1.0 / HEAD_DIM)
    gated = og * lax.rsqrt(ms + RMS_EPS) * gg_ref[...] * _silu(z_ref[0].astype(F32))
    y = _dot(ona_ref[0], w_ref[0:NA_W, :]) + _dot(gated.astype(BF16), w_ref[NA_W:NA_W + GDN_W, :])
    gate = mod_ref[0][:, 2 * d:3 * d]
    o_ref[0] = x_ref[0] + gate * _rms(y, gp_ref[...])


def _post_mixer(x_all, mod_l, o_na, o_f, o_b, z, gdn_gain, post_gain, w_out, ones_bd):
    nb, s_len, d = x_all.shape
    ns = s_len // TILE
    tok = lambda w: pl.BlockSpec((1, TILE, w), lambda b, s: (b, s, 0))
    return pl.pallas_call(
        _post_mixer_kernel,
        grid=(nb, ns),
        in_specs=[tok(d),
                  pl.BlockSpec((1, 1, N_MOD * d), lambda b, s: (_mod_row(b, s, nb), 0, 0)),
                  tok(NA_W), tok(GDN_W), tok(GDN_W), tok(GDN_W),
                  _const_spec(gdn_gain.shape), _const_spec(post_gain.shape),
                  _const_spec(w_out.shape), _const_spec(ones_bd.shape)],
        out_specs=tok(d),
        out_shape=jax.ShapeDtypeStruct(x_all.shape, F32),
        input_output_aliases={0: 0},
        compiler_params=_cparams(2),
        name="post_mixer",
    )(x_all, mod_l, o_na, o_f, o_b, z, gdn_gain, post_gain, w_out, ones_bd)


def _conv_ffn_kernel(x_ref, xp_ref, xn_ref, mod_ref, gpre_ref, gpost_ref, wu_ref, cw_ref, wd_ref,
                     o_ref):
    s = pl.program_id(1)
    ns = pl.num_programs(1)
    d = x_ref.shape[-1]
    d_ff = wd_ref.shape[0]
    m = mod_ref[0]
    shift, scale, gate = m[:, 3 * d:4 * d], m[:, 4 * d:5 * d], m[:, 5 * d:6 * d]
    x = x_ref[0]
    prev_ok = (s >= 2).astype(F32)
    next_ok = jnp.logical_and(s >= 1, s < ns - 1).astype(F32)

    def hidden(t):
        return _rms(t, gpre_ref[...]) * (1.0 + scale) + shift

    h = jnp.concatenate([hidden(xp_ref[0]) * prev_ok, hidden(x), hidden(xn_ref[0]) * next_ok],
                        axis=0).astype(BF16)
    acc = jnp.zeros((TILE, d), F32)
    for c in range(d_ff // FFN_COLS):
        halves = []
        for off in (c * FFN_COLS, d_ff + c * FFN_COLS):
            u = _dot(h, wu_ref[:, off:off + FFN_COLS])
            halves.append(cw_ref[0:1, off:off + FFN_COLS] * u[HALO - 1:HALO - 1 + TILE]
                          + cw_ref[1:2, off:off + FFN_COLS] * u[HALO:HALO + TILE]
                          + cw_ref[2:3, off:off + FFN_COLS] * u[HALO + 1:HALO + 1 + TILE])
        act = (_silu(halves[0]) * halves[1]).astype(BF16)
        acc = acc + _dot(act, wd_ref[c * FFN_COLS:(c + 1) * FFN_COLS, :])
    o_ref[0] = x + gate * _rms(acc, gpost_ref[...])


def _conv_ffn(x_all, mod_l, pre_gain, post_gain, w_up, conv_w, w_down):
    nb, s_len, d = x_all.shape
    ns = s_len // TILE
    hpt = TILE // HALO
    nh = s_len // HALO
    tok = pl.BlockSpec((1, TILE, d), lambda b, s: (b, s, 0))
    return pl.pallas_call(
        _conv_ffn_kernel,
        grid=(nb, ns),
        in_specs=[tok,
                  pl.BlockSpec((1, HALO, d), lambda b, s: (b, jnp.maximum(s * hpt - 1, 0), 0)),
                  pl.BlockSpec((1, HALO, d), lambda b, s: (b, jnp.minimum((s + 1) * hpt, nh - 1), 0)),
                  pl.BlockSpec((1, 1, N_MOD * d), lambda b, s: (_mod_row(b, s, nb), 0, 0)),
                  _const_spec(pre_gain.shape), _const_spec(post_gain.shape),
                  _const_spec(w_up.shape), _const_spec(conv_w.shape), _const_spec(w_down.shape)],
        out_specs=tok,
        out_shape=jax.ShapeDtypeStruct(x_all.shape, F32),
        compiler_params=_cparams(2),
        name="conv_ffn",
    )(x_all, x_all, x_all, mod_l, pre_gain, post_gain, w_up, conv_w, w_down)


def _rotary_tables(n_ctx, n_lat):
    t = jnp.arange(n_lat)
    row = (t // GRID_W).astype(F32)
    col = (t % GRID_W).astype(F32)
    pairs = HEAD_DIM // 4
    inv_freq = ROPE_BASE ** (-jnp.arange(pairs, dtype=F32) / pairs)
    ang = jnp.concatenate([row[:, None] * inv_freq, col[:, None] * inv_freq], axis=-1)
    cos, sin = jnp.cos(ang), jnp.sin(ang)
    cos64 = jnp.concatenate([cos, cos], axis=-1)
    sin64 = jnp.concatenate([-sin, sin], axis=-1)
    cos_t = jnp.concatenate([jnp.ones((n_ctx, HEAD_DIM), F32), cos64], axis=0)
    sin_t = jnp.concatenate([jnp.zeros((n_ctx, HEAD_DIM), F32), sin64], axis=0)
    return jnp.tile(cos_t, (1, 2)), jnp.tile(sin_t, (1, 2))


def _static_masks():
    lane_h = np.arange(GROUP_W) // HEAD_DIM
    bdmask = (lane_h[:, None] == lane_h[None, :]).astype(np.float32)
    seg = np.arange(GDN_W) // HEAD_DIM
    ones_bd = (seg[:, None] == seg[None, :]).astype(np.float32)
    i = np.arange(CHUNK)[:, None]
    j = (np.arange(GROUP_W) % HEAD_DIM)[None, :]
    lvl = []
    for lv in range(N_LEVELS):
        n = 2 ** lv
        lvl.append(np.logical_and(i // (2 * n) == j // (2 * n), i // n != j // n))
    lvl.append(i == j)
    lvl = np.stack(lvl).astype(np.float32)
    return jnp.asarray(bdmask, BF16), jnp.asarray(ones_bd, BF16), jnp.asarray(lvl, F32)


def _gate_lane_vector(p):
    flat = p.reshape(1, 2 * GDN_HEADS).astype(F32)
    return jnp.pad(flat, ((0, 0), (2 * GDN_HEADS, GATE_PAD - 4 * GDN_HEADS)))


def kernel(x, c, ctx, c_ctx, ada_w, ada_b, norm_mix_pre, norm_mix_post, w_in, qkv_conv, a_log,
           dt_bias, gdn_norm, rpb, w_out, norm_ffn_pre, norm_ffn_post, ffn_up, ffn_conv, ffn_down):
    nb, n_lat, d = x.shape
    n_ctx = ctx.shape[1]
    depth = ada_w.shape[0]
    assert n_ctx == TILE and n_lat % TILE == 0 and n_lat // GRID_W >= NA_KH

    mod_rows = -(-(nb + 1) // 8) * 8
    cc = jnp.concatenate([c, c_ctx[None, :], jnp.zeros((mod_rows - nb - 1, d), F32)], axis=0)
    mod_all = _modulation(cc, ada_w, ada_b)

    cos_t, sin_t = _rotary_tables(n_ctx, n_lat)
    bdmask, ones_bd, lvl = _static_masks()
    in_w = w_in.shape[-1]
    w_in_p = jnp.pad(w_in, ((0, 0), (0, 0), (0, GATE_PAD - (in_w - 3 * NA_W - 4 * GDN_W)))).astype(BF16)
    w_out_b = w_out.astype(BF16)
    w_up_b = ffn_up.astype(BF16)
    w_down_b = ffn_down.astype(BF16)

    x_all = jnp.concatenate([ctx, x], axis=1)
    for l in range(depth):
        mod_l = mod_all[l].reshape(mod_rows, 1, N_MOD * d)
        naq, nak, nav, gqkv, z, gates = _pre_mixer(x_all, mod_l, norm_mix_pre[l][None, :], w_in_p[l])
        u, w, qg, qk, kdt, eg = _gdn_local(gqkv, gates, qkv_conv[l], _gate_lane_vector(a_log[l]),
                                           _gate_lane_vector(dt_bias[l]), cos_t, sin_t, ones_bd,
                                           bdmask, lvl)
        o_f, o_b = _gdn_scan(u, w, qg, qk, kdt, eg, bdmask)
        o_na = _na_attention(naq, nak, nav, _na_bias_table(rpb[l]))
        x_all = _post_mixer(x_all, mod_l, o_na, o_f, o_b, z, jnp.tile(gdn_norm[l], GDN_HEADS)[None, :],
                            norm_mix_post[l][None, :], w_out_b[l], ones_bd)
        x_all = _conv_ffn(x_all, mod_l, norm_ffn_pre[l][None, :], norm_ffn_post[l][None, :],
                          w_up_b[l], ffn_conv[l], w_down_b[l])
    return x_all[:, n_ctx:]
```

```python
import functools

import jax
import jax.numpy as jnp
import numpy as np
from jax import lax
from jax.experimental import pallas as pl
from jax.experimental.pallas import tpu as pltpu

F32 = jnp.float32
BF16 = jnp.bfloat16

HEAD_DIM = 64
NA_HEADS = 8
GDN_HEADS = 8
NA_W = NA_HEADS * HEAD_DIM
GDN_W = GDN_HEADS * HEAD_DIM
GRID_W = 64
NA_KH = 8
NA_KW = 16
CHUNK = 64
ROPE_BASE = 10000.0
RMS_EPS = 1e-6
N_MOD = 6
TILE = 256
CHUNKS_PER_TILE = TILE // CHUNK
HALO = 8
GROUP_W = 128
HEADS_PER_GROUP = GROUP_W // HEAD_DIM
N_LEVELS = 6
LOCAL_PAR_CHUNKS = 2
MASK_NEG = -1e30
GATE_PAD = 128
FFN_COLS = 256
VMEM_LIMIT = 56 * 1024 * 1024


def _cparams(n_axes):
    return pltpu.CompilerParams(dimension_semantics=("arbitrary",) * n_axes,
                                vmem_limit_bytes=VMEM_LIMIT)


def _const_spec(shape):
    nd = len(shape)
    return pl.BlockSpec(shape, lambda *_: (0,) * nd)


def _sigmoid(x):
    return 1.0 / (1.0 + jnp.exp(-x))


def _silu(x):
    return x * _sigmoid(x)


def _softplus(x):
    return jnp.maximum(x, 0.0) + jnp.log(1.0 + jnp.exp(-jnp.abs(x)))


def _rms(x, gain):
    ms = jnp.mean(x * x, axis=-1, keepdims=True)
    return x * lax.rsqrt(ms + RMS_EPS) * gain


def _dot(a, b):
    return jnp.dot(a, b, preferred_element_type=F32)


def _dot_nt(a, b):
    return lax.dot_general(a, b, (((1,), (1,)), ((), ())), preferred_element_type=F32)


def _mod_row(b, s, n_batch):
    return jnp.where(s == 0, n_batch, b)


def _modulation_kernel(c_ref, w_ref, b_ref, o_ref):
    a = _silu(c_ref[...]).astype(BF16)
    o_ref[0] = _dot(a, w_ref[0].astype(BF16)) + b_ref[0]


def _modulation(cc, ada_w, ada_b):
    depth, d, n = ada_w.shape
    rows = cc.shape[0]
    nblk = n // N_MOD
    return pl.pallas_call(
        _modulation_kernel,
        grid=(depth, N_MOD),
        in_specs=[pl.BlockSpec((rows, d), lambda l, j: (0, 0)),
                  pl.BlockSpec((1, d, nblk), lambda l, j: (l, 0, j)),
                  pl.BlockSpec((1, 1, nblk), lambda l, j: (l, 0, j))],
        out_specs=pl.BlockSpec((1, rows, nblk), lambda l, j: (l, 0, j)),
        out_shape=jax.ShapeDtypeStruct((depth, rows, n), F32),
        compiler_params=_cparams(2),
        name="modulation",
    )(cc, ada_w, ada_b.reshape(depth, 1, n))


def _pre_mixer_kernel(x_ref, mod_ref, g_ref, w_ref, q_ref, k_ref, v_ref, gq_ref, z_ref, gt_ref):
    d = x_ref.shape[-1]
    m = mod_ref[0]
    h = _rms(x_ref[0], g_ref[...]) * (1.0 + m[:, d:2 * d]) + m[:, 0:d]
    hb = h.astype(BF16)
    o = 0
    q_ref[0] = (_dot(hb, w_ref[:, o:o + NA_W]) * HEAD_DIM ** -0.5).astype(BF16)
    o += NA_W
    k_ref[0] = _dot(hb, w_ref[:, o:o + NA_W]).astype(BF16)
    o += NA_W
    v_ref[0] = _dot(hb, w_ref[:, o:o + NA_W]).astype(BF16)
    o += NA_W
    gq_ref[0] = _dot(hb, w_ref[:, o:o + 3 * GDN_W]).astype(BF16)
    o += 3 * GDN_W
    z_ref[0] = _dot(hb, w_ref[:, o:o + GDN_W]).astype(BF16)
    o += GDN_W
    gt_ref[0] = _dot(hb, w_ref[:, o:o + GATE_PAD])


def _pre_mixer(x_all, mod_l, gain, w_in_p):
    nb, s_len, d = x_all.shape
    ns = s_len // TILE
    tok = lambda w: pl.BlockSpec((1, TILE, w), lambda b, s: (b, s, 0))
    shp = lambda w, dt: jax.ShapeDtypeStruct((nb, s_len, w), dt)
    return pl.pallas_call(
        _pre_mixer_kernel,
        grid=(nb, ns),
        in_specs=[tok(d),
                  pl.BlockSpec((1, 1, N_MOD * d), lambda b, s: (_mod_row(b, s, nb), 0, 0)),
                  _const_spec((1, d)),
                  _const_spec(w_in_p.shape)],
        out_specs=[tok(NA_W), tok(NA_W), tok(NA_W), tok(3 * GDN_W), tok(GDN_W), tok(GATE_PAD)],
        out_shape=[shp(NA_W, BF16), shp(NA_W, BF16), shp(NA_W, BF16), shp(3 * GDN_W, BF16),
                   shp(GDN_W, BF16), shp(GATE_PAD, F32)],
        compiler_params=_cparams(2),
        name="pre_mixer",
    )(x_all, mod_l, gain, w_in_p)


def _block_diag(x, bdmask):
    return jnp.concatenate([x] * HEADS_PER_GROUP, axis=0) * bdmask


def _split3(x):
    a = x.astype(BF16)
    r = x - a.astype(F32)
    b = r.astype(BF16)
    c = (r - b.astype(F32)).astype(BF16)
    return a, b, c


def _expand_heads(a, c0):
    rows = a.shape[0]
    lane = lax.broadcasted_iota(jnp.int32, (rows, 128), 1)
    parts = []
    for p in range(GDN_HEADS // 2):
        lo = jnp.broadcast_to(a[:, c0 + 2 * p:c0 + 2 * p + 1], (rows, 128))
        hi = jnp.broadcast_to(a[:, c0 + 2 * p + 1:c0 + 2 * p + 2], (rows, 128))
        parts.append(jnp.where(lane < HEAD_DIM, lo, hi))
    return jnp.concatenate(parts, axis=1)


def _gdn_local_kernel(x_ref, xp_ref, xn_ref, gt_ref, cw_ref, av_ref, dt_ref, cos_ref, sin_ref,
                      ones_ref, bd_ref, lvl_ref,
                      u_ref, w_ref, qg_ref, qk_ref, kdt_ref, eg_ref,
                      q_s, k_s, v_s, gc_s, be_s):
    s = pl.program_id(1)
    ns = pl.num_programs(1)
    width = x_ref.shape[-1]

    x = x_ref[0].astype(F32)
    prev_ok = (s >= 2).astype(F32)
    next_ok = jnp.logical_and(s >= 1, s < ns - 1).astype(F32)
    prev_row = xp_ref[0, HALO - 1:HALO, :].astype(F32) * prev_ok
    next_row = xn_ref[0, 0:1, :].astype(F32) * next_ok
    ri = lax.broadcasted_iota(jnp.int32, (TILE, width), 0)
    x_m1 = jnp.where(ri == 0, prev_row, pltpu.roll(x, 1, 0))
    x_p1 = jnp.where(ri == TILE - 1, next_row, pltpu.roll(x, TILE - 1, 0))
    y = cw_ref[0:1, :] * x_m1 + cw_ref[1:2, :] * x + cw_ref[2:3, :] * x_p1
    a = _silu(y)
    q = a[:, 0:GDN_W]
    k = a[:, GDN_W:2 * GDN_W]
    v_s[...] = a[:, 2 * GDN_W:3 * GDN_W]

    ones_bd = ones_ref[...]

    def seg_sum(t):
        hi = t.astype(BF16)
        lo = (t - hi.astype(F32)).astype(BF16)
        return _dot(hi, ones_bd) + _dot(lo, ones_bd)

    q = q * lax.rsqrt(seg_sum(q * q) + RMS_EPS)
    k = k * lax.rsqrt(seg_sum(k * k) + RMS_EPS)
    cos_t = cos_ref[...]
    sin_t = sin_ref[...]
    lane = lax.broadcasted_iota(jnp.int32, (TILE, 128), 1)
    first_half = (lane % HEAD_DIM) < HEAD_DIM // 2

    def rope(t):
        parts = []
        for p in range(GDN_W // 128):
            ts = t[:, 128 * p:128 * (p + 1)]
            partner = jnp.where(first_half, pltpu.roll(ts, 128 - HEAD_DIM // 2, 1),
                                pltpu.roll(ts, HEAD_DIM // 2, 1))
            parts.append(ts * cos_t + partner * sin_t)
        return jnp.concatenate(parts, axis=1)

    q_s[...] = rope(q) * HEAD_DIM ** -0.5
    k_s[...] = rope(k)

    gt = gt_ref[0]
    beta = _sigmoid(gt)
    g_raw = -jnp.exp(av_ref[...]) * _softplus(gt + dt_ref[...])
    r_i = lax.broadcasted_iota(jnp.int32, (TILE, TILE), 0)
    c_i = lax.broadcasted_iota(jnp.int32, (TILE, TILE), 1)
    same_chunk = (r_i // CHUNK) == (c_i // CHUNK)
    tri_f = jnp.where(jnp.logical_and(same_chunk, r_i >= c_i), 1.0, 0.0).astype(BF16)
    tri_b = jnp.where(jnp.logical_and(same_chunk, r_i <= c_i), 1.0, 0.0).astype(BF16)
    g1, g2, g3 = _split3(g_raw)
    cum_f = _dot(tri_f, g1) + _dot(tri_f, g2) + _dot(tri_f, g3)
    cum_b = _dot(tri_b, g1) + _dot(tri_b, g2) + _dot(tri_b, g3)
    gc_s[0] = _expand_heads(cum_f, 2 * GDN_HEADS)
    gc_s[1] = _expand_heads(cum_b, 3 * GDN_HEADS)
    be_s[0] = _expand_heads(beta, 0)
    be_s[1] = _expand_heads(beta, GDN_HEADS)

    bdmask = bd_ref[...]
    eye = lvl_ref[N_LEVELS]
    ii = lax.broadcasted_iota(jnp.int32, (CHUNK, GROUP_W), 0)
    jj = lax.broadcasted_iota(jnp.int32, (CHUNK, GROUP_W), 1) % HEAD_DIM

    def sdot(a_stack, b_stack):
        return _dot(a_stack.astype(BF16), _block_diag(b_stack.astype(BF16), bdmask))

    def setup(c, d, gi):
        rows = slice(c * CHUNK, (c + 1) * CHUNK)
        lanes = slice(gi * GROUP_W, (gi + 1) * GROUP_W)
        kk = k_s[rows, lanes]
        qq = q_s[rows, lanes]
        gc = gc_s[d, rows, lanes]
        be = be_s[d, rows, lanes]
        tri = (ii >= jj) if d == 0 else (ii <= jj)
        g_last = gc[CHUNK - 1:CHUNK, :] if d == 0 else gc[0:1, :]
        kb = kk * be
        e_gc = jnp.exp(gc)
        gc_t = jnp.sum(gc * eye, axis=0, keepdims=True)
        dec = jnp.exp(jnp.where(tri, gc - gc_t, MASK_NEG))
        qg_ref[d, 0, rows, lanes] = (qq * e_gc).astype(BF16)
        return dict(c=c, rows=rows, lanes=lanes, d=d, kk=kk, dec=dec, g_last=g_last,
                    lhs=jnp.concatenate([kb, qq], axis=0).astype(BF16),
                    vb=(v_s[rows, lanes] * be).astype(BF16), kbg=(kb * e_gc).astype(BF16),
                    kd=(kk * jnp.exp(g_last - gc)).astype(BF16))

    for c0 in range(0, CHUNKS_PER_TILE, LOCAL_PAR_CHUNKS):
        ps = [setup(c, d, gi) for c in range(c0, c0 + LOCAL_PAR_CHUNKS) for d in range(2)
              for gi in range(GDN_W // GROUP_W)]
        grams = [_dot_nt(p["lhs"], _block_diag(p["kk"].astype(BF16), bdmask)) for p in ps]
        ms = []
        for p, gram in zip(ps, grams):
            ms.append(gram[0:CHUNK] * p["dec"] * (1.0 - eye))
            qk_ref[p["d"], 0, p["rows"], p["lanes"]] = (gram[CHUNK:2 * CHUNK] * p["dec"]).astype(BF16)
        xs = [eye - m * lvl_ref[0] for m in ms]
        for lv in range(1, N_LEVELS):
            ys = [sdot(x, m * lvl_ref[lv]) for x, m in zip(xs, ms)]
            zs = [sdot(y, x) for y, x in zip(ys, xs)]
            xs = [x - z for x, z in zip(xs, zs)]
        xb = [x.astype(BF16) for x in xs]
        us = [_dot(x, _block_diag(p["vb"], bdmask)) for x, p in zip(xb, ps)]
        ws = [_dot(x, _block_diag(p["kbg"], bdmask)) for x, p in zip(xb, ps)]
        kdts = [_dot_nt(eye.astype(BF16), _block_diag(p["kd"], bdmask)) for p in ps]
        for p, u, w, kdt in zip(ps, us, ws, kdts):
            u_ref[p["d"], 0, p["rows"], p["lanes"]] = u.astype(BF16)
            w_ref[p["d"], 0, p["rows"], p["lanes"]] = w.astype(BF16)
            kdt_ref[p["d"], 0, p["rows"], p["lanes"]] = kdt.astype(BF16)
            eg_ref[p["d"], 0, 0, p["c"]:p["c"] + 1, p["lanes"]] = jnp.exp(p["g_last"])


def _gdn_local(gqkv, gates, conv_w, avec, dtvec, cos_t, sin_t, ones_bd, bdmask, lvl):
    nb, s_len, width = gqkv.shape
    ns = s_len // TILE
    hpt = TILE // HALO
    nh = s_len // HALO
    dir_spec = pl.BlockSpec((2, 1, TILE, GDN_W), lambda b, s: (0, b, s, 0))
    dir_shape = jax.ShapeDtypeStruct((2, nb, s_len, GDN_W), BF16)
    return pl.pallas_call(
        _gdn_local_kernel,
        grid=(nb, ns),
        in_specs=[pl.BlockSpec((1, TILE, width), lambda b, s: (b, s, 0)),
                  pl.BlockSpec((1, HALO, width), lambda b, s: (b, jnp.maximum(s * hpt - 1, 0), 0)),
                  pl.BlockSpec((1, HALO, width),
                               lambda b, s: (b, jnp.minimum((s + 1) * hpt, nh - 1), 0)),
                  pl.BlockSpec((1, TILE, GATE_PAD), lambda b, s: (b, s, 0)),
                  _const_spec(conv_w.shape),
                  _const_spec(avec.shape),
                  _const_spec(dtvec.shape),
                  pl.BlockSpec((TILE, 128), lambda b, s: (s, 0)),
                  pl.BlockSpec((TILE, 128), lambda b, s: (s, 0)),
                  _const_spec(ones_bd.shape),
                  _const_spec(bdmask.shape),
                  _const_spec(lvl.shape)],
        out_specs=[dir_spec] * 5 + [pl.BlockSpec((2, 1, 1, CHUNKS_PER_TILE, GDN_W),
                                                 lambda b, s: (0, b, s, 0, 0))],
        out_shape=[dir_shape] * 5 + [jax.ShapeDtypeStruct((2, nb, ns, CHUNKS_PER_TILE, GDN_W), F32)],
        scratch_shapes=[pltpu.VMEM((TILE, GDN_W), F32), pltpu.VMEM((TILE, GDN_W), F32),
                        pltpu.VMEM((TILE, GDN_W), F32), pltpu.VMEM((2, TILE, GDN_W), F32),
                        pltpu.VMEM((2, TILE, GDN_W), F32)],
        compiler_params=_cparams(2),
        name="gdn_local",
    )(gqkv, gqkv, gqkv, gates, conv_w, avec, dtvec, cos_t, sin_t, ones_bd, bdmask, lvl)


def _gdn_scan_kernel(uf, wf, qgf, qkf, kdf, egf, ub, wb, qgb, qkb, kdb, egb, bd_ref,
                     of_ref, ob_ref, state):
    j = pl.program_id(1)

    @pl.when(j == 0)
    def _():
        state[...] = jnp.zeros_like(state)

    bdmask = bd_ref[...]
    ins = ((uf, wf, qgf, qkf, kdf, egf, of_ref), (ub, wb, qgb, qkb, kdb, egb, ob_ref))
    n_groups = GDN_W // GROUP_W
    chains = [(d, gi) for d in range(2) for gi in range(n_groups)]
    sts = [state[n_groups * d + gi] for d, gi in chains]
    for ci in range(CHUNKS_PER_TILE):
        sl = []
        for d, gi in chains:
            c = ci if d == 0 else CHUNKS_PER_TILE - 1 - ci
            sl.append((c, slice(c * CHUNK, (c + 1) * CHUNK), slice(gi * GROUP_W, (gi + 1) * GROUP_W)))
        r1s = []
        for (d, gi), (c, rows, lanes), st in zip(chains, sl, sts):
            w_r, qg_r = ins[d][1], ins[d][2]
            lhs1 = jnp.concatenate([w_r[0, 0, rows, lanes], qg_r[0, 0, rows, lanes]], axis=0)
            r1s.append(_dot(lhs1, _block_diag(st.astype(BF16), bdmask)))
        r2s = []
        for (d, gi), (c, rows, lanes), r1 in zip(chains, sl, r1s):
            u_r, qk_r, kd_r = ins[d][0], ins[d][3], ins[d][4]
            v_new = u_r[0, 0, rows, lanes].astype(F32) - r1[0:CHUNK]
            lhs2 = jnp.concatenate([qk_r[0, 0, rows, lanes], kd_r[0, 0, rows, lanes]], axis=0)
            r2s.append(_dot(lhs2, _block_diag(v_new.astype(BF16), bdmask)))
        new_sts = []
        for (d, gi), (c, rows, lanes), st, r1, r2 in zip(chains, sl, sts, r1s, r2s):
            eg_r, o_r = ins[d][5], ins[d][6]
            o_r[0, rows, lanes] = (r1[CHUNK:2 * CHUNK] + r2[0:CHUNK]).astype(BF16)
            new_sts.append(st * eg_r[0, 0, 0, c:c + 1, lanes] + r2[CHUNK:2 * CHUNK])
        sts = new_sts
    for (d, gi), st in zip(chains, sts):
        state[n_groups * d + gi] = st


def _gdn_scan(u, w, qg, qk, kdt, eg, bdmask):
    _, nb, s_len, _ = u.shape
    ns = s_len // TILE
    bwd = lambda j: jnp.where(j == 0, 0, ns - j)
    f_spec = pl.BlockSpec((1, 1, TILE, GDN_W), lambda b, j: (0, b, j, 0))
    b_spec = pl.BlockSpec((1, 1, TILE, GDN_W), lambda b, j: (1, b, bwd(j), 0))
    egf_spec = pl.BlockSpec((1, 1, 1, CHUNKS_PER_TILE, GDN_W), lambda b, j: (0, b, j, 0, 0))
    egb_spec = pl.BlockSpec((1, 1, 1, CHUNKS_PER_TILE, GDN_W), lambda b, j: (1, b, bwd(j), 0, 0))
    out_shape = jax.ShapeDtypeStruct((nb, s_len, GDN_W), BF16)
    return pl.pallas_call(
        _gdn_scan_kernel,
        grid=(nb, ns),
        in_specs=[f_spec] * 5 + [egf_spec] + [b_spec] * 5 + [egb_spec] + [_const_spec(bdmask.shape)],
        out_specs=[pl.BlockSpec((1, TILE, GDN_W), lambda b, j: (b, j, 0)),
                   pl.BlockSpec((1, TILE, GDN_W), lambda b, j: (b, bwd(j), 0))],
        out_shape=[out_shape, out_shape],
        scratch_shapes=[pltpu.VMEM((2 * GDN_W // GROUP_W, CHUNK, GROUP_W), F32)],
        compiler_params=_cparams(2),
        name="gdn_scan",
    )(u, w, qg, qk, kdt, eg, u, w, qg, qk, kdt, eg, bdmask)


def _na_kernel(q_ref, k_ref, v_ref, bias_ref, o_ref, *, n_rows):
    s = pl.program_id(2)
    lane = lax.broadcasted_iota(jnp.int32, (1, 128), 1)
    head_mask = (lane < HEAD_DIM, lane >= HEAD_DIM)

    @pl.when(s == 0)
    def _():
        q = q_ref[0]
        kc = k_ref[0, 0:TILE, :]
        vc = v_ref[0, 0:TILE, :]
        outs = []
        for h in range(2):
            qh = jnp.where(head_mask[h], q, jnp.zeros_like(q))
            sc = _dot_nt(qh, kc)
            p = jnp.exp(sc - jnp.max(sc, axis=-1, keepdims=True))
            outs.append(_dot(p.astype(BF16), vc) / jnp.sum(p, axis=-1, keepdims=True))
        o_ref[0] = jnp.where(head_mask[0], outs[0], outs[1]).astype(BF16)

    @pl.when(s > 0)
    def _():
        kc = k_ref[0, 0:TILE, :]
        vc = v_ref[0, 0:TILE, :]
        rows_per_tile = TILE // GRID_W
        q2, kws, vws, biases = [], [], [], []
        for r in range(rows_per_tile):
            row = (s - 1) * rows_per_tile + r
            rs = jnp.clip(row - NA_KH // 2, 0, n_rows - NA_KH)
            cls = rs - row + (NA_KH - 1)
            start = pl.multiple_of(TILE + rs * GRID_W, GRID_W)
            kws.append(k_ref[0, pl.ds(start, NA_KH * GRID_W), :])
            vws.append(v_ref[0, pl.ds(start, NA_KH * GRID_W), :])
            biases.append(jnp.concatenate([bias_ref[0, cls], bias_ref[1, cls]], axis=0))
            q = q_ref[0, r * GRID_W:(r + 1) * GRID_W, :]
            q2.append(jnp.concatenate([jnp.where(head_mask[h], q, jnp.zeros_like(q))
                                       for h in range(2)], axis=0))
        sc_all = _dot_nt(jnp.concatenate(q2, axis=0), kc)
        sws = [_dot_nt(q2[r], kws[r]) + biases[r] for r in range(rows_per_tile)]
        pws, pcs, dens = [], [], []
        for r in range(rows_per_tile):
            sc = sc_all[2 * GRID_W * r:2 * GRID_W * (r + 1)]
            mx = jnp.maximum(jnp.max(sws[r], axis=-1, keepdims=True),
                             jnp.max(sc, axis=-1, keepdims=True))
            pw = jnp.exp(sws[r] - mx)
            pc = jnp.exp(sc - mx)
            dens.append(jnp.sum(pw, axis=-1, keepdims=True) + jnp.sum(pc, axis=-1, keepdims=True))
            pws.append(pw.astype(BF16))
            pcs.append(pc.astype(BF16))
        oc_all = _dot(jnp.concatenate(pcs, axis=0), vc)
        results = []
        for r in range(rows_per_tile):
            o2 = (_dot(pws[r], vws[r]) + oc_all[2 * GRID_W * r:2 * GRID_W * (r + 1)]) / dens[r]
            results.append(jnp.where(head_mask[0], o2[0:GRID_W], o2[GRID_W:2 * GRID_W]).astype(BF16))
        o_ref[0] = jnp.concatenate(results, axis=0)


def _na_attention(q, k, v, bias_tab):
    nb, s_len, _ = q.shape
    ns = s_len // TILE
    n_rows = (s_len - TILE) // GRID_W
    return pl.pallas_call(
        functools.partial(_na_kernel, n_rows=n_rows),
        grid=(nb, NA_W // 128, ns),
        in_specs=[pl.BlockSpec((1, TILE, 128), lambda b, hp, s: (b, s, hp)),
                  pl.BlockSpec((1, s_len, 128), lambda b, hp, s: (b, 0, hp)),
                  pl.BlockSpec((1, s_len, 128), lambda b, hp, s: (b, 0, hp)),
                  pl.BlockSpec((2, NA_KH, GRID_W, NA_KH * GRID_W), lambda b, hp, s: (hp, 0, 0, 0))],
        out_specs=pl.BlockSpec((1, TILE, 128), lambda b, hp, s: (b, s, hp)),
        out_shape=jax.ShapeDtypeStruct((nb, s_len, NA_W), BF16),
        compiler_params=_cparams(3),
        name="na_attn",
    )(q, k, v, bias_tab)


def _na_bias_table(rpb):
    qi = jnp.arange(GRID_W)[:, None]
    kc = jnp.arange(GRID_W)[None, :]
    cs = jnp.clip(qi - NA_KW // 2, 0, GRID_W - NA_KW)
    in_win = jnp.logical_and(kc >= cs, kc < cs + NA_KW)
    dc = jnp.clip(kc - qi + (NA_KW - 1), 0, 2 * NA_KW - 2)
    dr = jnp.arange(NA_KH)[:, None] + jnp.arange(NA_KH)[None, :]
    tab = rpb[:, dr][:, :, :, dc]
    tab = jnp.where(in_win[None, None, None], tab, MASK_NEG)
    tab = jnp.transpose(tab, (0, 1, 3, 2, 4))
    return tab.reshape(rpb.shape[0], NA_KH, GRID_W, NA_KH * GRID_W).astype(F32)


def _post_mixer_kernel(x_ref, mod_ref, ona_ref, of_ref, ob_ref, z_ref, gg_ref, gp_ref, w_ref,
                       ones_ref, o_ref):
    d = x_ref.shape[-1]
    og = of_ref[0].astype(F32) + ob_ref[0].astype(F32)
    sq = og * og
    hi = sq.astype(BF16)
    lo = (sq - hi.astype(F32)).astype(BF16)
    ms = (_dot(hi, ones_ref[...]) + _dot(lo, ones_ref[...])) * (1.0 / HEAD_DIM)
    gated = og * lax.rsqrt(ms + RMS_EPS) * gg_ref[...] * _silu(z_ref[0].astype(F32))
    y = _dot(ona_ref[0], w_ref[0:NA_W, :]) + _dot(gated.astype(BF16), w_ref[NA_W:NA_W + GDN_W, :])
    gate = mod_ref[0][:, 2 * d:3 * d]
    o_ref[0] = x_ref[0] + gate * _rms(y, gp_ref[...])


def _post_mixer(x_all, mod_l, o_na, o_f, o_b, z, gdn_gain, post_gain, w_out, ones_bd):
    nb, s_len, d = x_all.shape
    ns = s_len // TILE
    tok = lambda w: pl.BlockSpec((1, TILE, w), lambda b, s: (b, s, 0))
    return pl.pallas_call(
        _post_mixer_kernel,
        grid=(nb, ns),
        in_specs=[tok(d),
                  pl.BlockSpec((1, 1, N_MOD * d), lambda b, s: (_mod_row(b, s, nb), 0, 0)),
                  tok(NA_W), tok(GDN_W), tok(GDN_W), tok(GDN_W),
                  _const_spec(gdn_gain.shape), _const_spec(post_gain.shape),
                  _const_spec(w_out.shape), _const_spec(ones_bd.shape)],
        out_specs=tok(d),
        out_shape=jax.ShapeDtypeStruct(x_all.shape, F32),
        input_output_aliases={0: 0},
        compiler_params=_cparams(2),
        name="post_mixer",
    )(x_all, mod_l, o_na, o_f, o_b, z, gdn_gain, post_gain, w_out, ones_bd)


def _conv_ffn_kernel(x_ref, xp_ref, xn_ref, mod_ref, gpre_ref, gpost_ref, wu_ref, cw_ref, wd_ref,
                     o_ref):
    s = pl.program_id(1)
    ns = pl.num_programs(1)
    d = x_ref.shape[-1]
    d_ff = wd_ref.shape[0]
    m = mod_ref[0]
    shift, scale, gate = m[:, 3 * d:4 * d], m[:, 4 * d:5 * d], m[:, 5 * d:6 * d]
    x = x_ref[0]
    prev_ok = (s >= 2).astype(F32)
    next_ok = jnp.logical_and(s >= 1, s < ns - 1).astype(F32)

    def hidden(t):
        return _rms(t, gpre_ref[...]) * (1.0 + scale) + shift

    h = jnp.concatenate([hidden(xp_ref[0]) * prev_ok, hidden(x), hidden(xn_ref[0]) * next_ok],
                        axis=0).astype(BF16)
    n_steps = d_ff // FFN_COLS

    def up(c):
        return [_dot(h, wu_ref[:, off:off + FFN_COLS]) for off in (c * FFN_COLS, d_ff + c * FFN_COLS)]

    def activation(c, us):
        halves = []
        for off, u in zip((c * FFN_COLS, d_ff + c * FFN_COLS), us):
            halves.append(cw_ref[0:1, off:off + FFN_COLS] * u[HALO - 1:HALO - 1 + TILE]
                          + cw_ref[1:2, off:off + FFN_COLS] * u[HALO:HALO + TILE]
                          + cw_ref[2:3, off:off + FFN_COLS] * u[HALO + 1:HALO + 1 + TILE])
        return (_silu(halves[0]) * halves[1]).astype(BF16)

    acc = jnp.zeros((TILE, d), F32)
    u_next = up(0)
    for c in range(n_steps):
        u_cur = u_next
        if c + 1 < n_steps:
            u_next = up(c + 1)
        acc = acc + _dot(activation(c, u_cur), wd_ref[c * FFN_COLS:(c + 1) * FFN_COLS, :])
    o_ref[0] = x + gate * _rms(acc, gpost_ref[...])


def _conv_ffn(x_all, mod_l, pre_gain, post_gain, w_up, conv_w, w_down):
    nb, s_len, d = x_all.shape
    ns = s_len // TILE
    hpt = TILE // HALO
    nh = s_len // HALO
    tok = pl.BlockSpec((1, TILE, d), lambda b, s: (b, s, 0))
    return pl.pallas_call(
        _conv_ffn_kernel,
        grid=(nb, ns),
        in_specs=[tok,
                  pl.BlockSpec((1, HALO, d), lambda b, s: (b, jnp.maximum(s * hpt - 1, 0), 0)),
                  pl.BlockSpec((1, HALO, d), lambda b, s: (b, jnp.minimum((s + 1) * hpt, nh - 1), 0)),
                  pl.BlockSpec((1, 1, N_MOD * d), lambda b, s: (_mod_row(b, s, nb), 0, 0)),
                  _const_spec(pre_gain.shape), _const_spec(post_gain.shape),
                  _const_spec(w_up.shape), _const_spec(conv_w.shape), _const_spec(w_down.shape)],
        out_specs=tok,
        out_shape=jax.ShapeDtypeStruct(x_all.shape, F32),
        compiler_params=_cparams(2),
        name="conv_ffn",
    )(x_all, x_all, x_all, mod_l, pre_gain, post_gain, w_up, conv_w, w_down)


def _rotary_tables(n_ctx, n_lat):
    t = jnp.arange(n_lat)
    row = (t // GRID_W).astype(F32)
    col = (t % GRID_W).astype(F32)
    pairs = HEAD_DIM // 4
    inv_freq = ROPE_BASE ** (-jnp.arange(pairs, dtype=F32) / pairs)
    ang = jnp.concatenate([row[:, None] * inv_freq, col[:, None] * inv_freq], axis=-1)
    cos, sin = jnp.cos(ang), jnp.sin(ang)
    cos64 = jnp.concatenate([cos, cos], axis=-1)
    sin64 = jnp.concatenate([-sin, sin], axis=-1)
    cos_t = jnp.concatenate([jnp.ones((n_ctx, HEAD_DIM), F32), cos64], axis=0)
    sin_t = jnp.concatenate([jnp.zeros((n_ctx, HEAD_DIM), F32), sin64], axis=0)
    return jnp.tile(cos_t, (1, 2)), jnp.tile(sin_t, (1, 2))


def _static_masks():
    lane_h = np.arange(GROUP_W) // HEAD_DIM
    bdmask = (lane_h[:, None] == lane_h[None, :]).astype(np.float32)
    seg = np.arange(GDN_W) // HEAD_DIM
    ones_bd = (seg[:, None] == seg[None, :]).astype(np.float32)
    i = np.arange(CHUNK)[:, None]
    j = (np.arange(GROUP_W) % HEAD_DIM)[None, :]
    lvl = []
    for lv in range(N_LEVELS):
        n = 2 ** lv
        lvl.append(np.logical_and(i // (2 * n) == j // (2 * n), i // n != j // n))
    lvl.append(i == j)
    lvl = np.stack(lvl).astype(np.float32)
    return jnp.asarray(bdmask, BF16), jnp.asarray(ones_bd, BF16), jnp.asarray(lvl, F32)


def _gate_lane_vector(p):
    flat = p.reshape(1, 2 * GDN_HEADS).astype(F32)
    return jnp.pad(flat, ((0, 0), (2 * GDN_HEADS, GATE_PAD - 4 * GDN_HEADS)))


def kernel(x, c, ctx, c_ctx, ada_w, ada_b, norm_mix_pre, norm_mix_post, w_in, qkv_conv, a_log,
           dt_bias, gdn_norm, rpb, w_out, norm_ffn_pre, norm_ffn_post, ffn_up, ffn_conv, ffn_down):
    nb, n_lat, d = x.shape
    n_ctx = ctx.shape[1]
    depth = ada_w.shape[0]
    assert n_ctx == TILE and n_lat % TILE == 0 and n_lat // GRID_W >= NA_KH

    mod_rows = -(-(nb + 1) // 8) * 8
    cc = jnp.concatenate([c, c_ctx[None, :], jnp.zeros((mod_rows - nb - 1, d), F32)], axis=0)
    mod_all = _modulation(cc, ada_w, ada_b)

    cos_t, sin_t = _rotary_tables(n_ctx, n_lat)
    bdmask, ones_bd, lvl = _static_masks()
    in_w = w_in.shape[-1]
    w_in_p = jnp.pad(w_in, ((0, 0), (0, 0), (0, GATE_PAD - (in_w - 3 * NA_W - 4 * GDN_W)))).astype(BF16)
    w_out_b = w_out.astype(BF16)
    w_up_b = ffn_up.astype(BF16)
    w_down_b = ffn_down.astype(BF16)

    x_all = jnp.concatenate([ctx, x], axis=1)
    for l in range(depth):
        mod_l = mod_all[l].reshape(mod_rows, 1, N_MOD * d)
        naq, nak, nav, gqkv, z, gates = _pre_mixer(x_all, mod_l, norm_mix_pre[l][None, :], w_in_p[l])
        u, w, qg, qk, kdt, eg = _gdn_local(gqkv, gates, qkv_conv[l], _gate_lane_vector(a_log[l]),
                                           _gate_lane_vector(dt_bias[l]), cos_t, sin_t, ones_bd,
                                           bdmask, lvl)
        o_f, o_b = _gdn_scan(u, w, qg, qk, kdt, eg, bdmask)
        o_na = _na_attention(naq, nak, nav, _na_bias_table(rpb[l]))
        x_all = _post_mixer(x_all, mod_l, o_na, o_f, o_b, z, jnp.tile(gdn_norm[l], GDN_HEADS)[None, :],
                            norm_mix_post[l][None, :], w_out_b[l], ones_bd)
        x_all = _conv_ffn(x_all, mod_l, norm_ffn_pre[l][None, :], norm_ffn_post[l][None, :],
                          w_up_b[l], ffn_conv[l], w_down_b[l])
    return x_all[:, n_ctx:]
```

```python
import functools

import jax
import jax.numpy as jnp
import numpy as np
from jax import lax
from jax.experimental import pallas as pl
from jax.experimental.pallas import tpu as pltpu

F32 = jnp.float32
BF16 = jnp.bfloat16

HEAD_DIM = 64
NA_HEADS = 8
GDN_HEADS = 8
NA_W = NA_HEADS * HEAD_DIM
GDN_W = GDN_HEADS * HEAD_DIM
GRID_W = 64
NA_KH = 8
NA_KW = 16
CHUNK = 64
ROPE_BASE = 10000.0
RMS_EPS = 1e-6
N_MOD = 6
TILE = 256
CHUNKS_PER_TILE = TILE // CHUNK
HALO = 8
GROUP_W = 128
HEADS_PER_GROUP = GROUP_W // HEAD_DIM
N_LEVELS = 6
LOCAL_PAR_CHUNKS = 2
SCAN_BATCH = 2
MASK_NEG = -1e30
GATE_PAD = 128
FFN_COLS = 256
VMEM_LIMIT = 56 * 1024 * 1024


def _cparams(n_axes):
    return pltpu.CompilerParams(dimension_semantics=("arbitrary",) * n_axes,
                                vmem_limit_bytes=VMEM_LIMIT)


def _const_spec(shape):
    nd = len(shape)
    return pl.BlockSpec(shape, lambda *_: (0,) * nd)


def _sigmoid(x):
    return 1.0 / (1.0 + jnp.exp(-x))


def _silu(x):
    return x * _sigmoid(x)


def _softplus(x):
    return jnp.maximum(x, 0.0) + jnp.log(1.0 + jnp.exp(-jnp.abs(x)))


def _rms(x, gain):
    ms = jnp.mean(x * x, axis=-1, keepdims=True)
    return x * lax.rsqrt(ms + RMS_EPS) * gain


def _dot(a, b):
    return jnp.dot(a, b, preferred_element_type=F32)


def _dot_nt(a, b):
    return lax.dot_general(a, b, (((1,), (1,)), ((), ())), preferred_element_type=F32)


def _mod_row(b, s, n_batch):
    return jnp.where(s == 0, n_batch, b)


def _modulation_kernel(c_ref, w_ref, b_ref, o_ref):
    a = _silu(c_ref[...]).astype(BF16)
    o_ref[0] = _dot(a, w_ref[0].astype(BF16)) + b_ref[0]


def _modulation(cc, ada_w, ada_b):
    depth, d, n = ada_w.shape
    rows = cc.shape[0]
    nblk = n // N_MOD
    return pl.pallas_call(
        _modulation_kernel,
        grid=(depth, N_MOD),
        in_specs=[pl.BlockSpec((rows, d), lambda l, j: (0, 0)),
                  pl.BlockSpec((1, d, nblk), lambda l, j: (l, 0, j)),
                  pl.BlockSpec((1, 1, nblk), lambda l, j: (l, 0, j))],
        out_specs=pl.BlockSpec((1, rows, nblk), lambda l, j: (l, 0, j)),
        out_shape=jax.ShapeDtypeStruct((depth, rows, n), F32),
        compiler_params=_cparams(2),
        name="modulation",
    )(cc, ada_w, ada_b.reshape(depth, 1, n))


def _pre_mixer_kernel(x_ref, mod_ref, g_ref, w_ref, q_ref, k_ref, v_ref, gq_ref, z_ref, gt_ref):
    d = x_ref.shape[-1]
    m = mod_ref[0]
    h = _rms(x_ref[0], g_ref[...]) * (1.0 + m[:, d:2 * d]) + m[:, 0:d]
    hb = h.astype(BF16)
    o = 0
    q_ref[0] = (_dot(hb, w_ref[:, o:o + NA_W]) * HEAD_DIM ** -0.5).astype(BF16)
    o += NA_W
    k_ref[0] = _dot(hb, w_ref[:, o:o + NA_W]).astype(BF16)
    o += NA_W
    v_ref[0] = _dot(hb, w_ref[:, o:o + NA_W]).astype(BF16)
    o += NA_W
    gq_ref[0] = _dot(hb, w_ref[:, o:o + 3 * GDN_W]).astype(BF16)
    o += 3 * GDN_W
    z_ref[0] = _dot(hb, w_ref[:, o:o + GDN_W]).astype(BF16)
    o += GDN_W
    gt_ref[0] = _dot(hb, w_ref[:, o:o + GATE_PAD])


def _pre_mixer(x_all, mod_l, gain, w_in_p):
    nb, s_len, d = x_all.shape
    ns = s_len // TILE
    tok = lambda w: pl.BlockSpec((1, TILE, w), lambda b, s: (b, s, 0))
    shp = lambda w, dt: jax.ShapeDtypeStruct((nb, s_len, w), dt)
    return pl.pallas_call(
        _pre_mixer_kernel,
        grid=(nb, ns),
        in_specs=[tok(d),
                  pl.BlockSpec((1, 1, N_MOD * d), lambda b, s: (_mod_row(b, s, nb), 0, 0)),
                  _const_spec((1, d)),
                  _const_spec(w_in_p.shape)],
        out_specs=[tok(NA_W), tok(NA_W), tok(NA_W), tok(3 * GDN_W), tok(GDN_W), tok(GATE_PAD)],
        out_shape=[shp(NA_W, BF16), shp(NA_W, BF16), shp(NA_W, BF16), shp(3 * GDN_W, BF16),
                   shp(GDN_W, BF16), shp(GATE_PAD, F32)],
        compiler_params=_cparams(2),
        name="pre_mixer",
    )(x_all, mod_l, gain, w_in_p)


def _block_diag(x, bdmask):
    return jnp.concatenate([x] * HEADS_PER_GROUP, axis=0) * bdmask


def _split3(x):
    a = x.astype(BF16)
    r = x - a.astype(F32)
    b = r.astype(BF16)
    c = (r - b.astype(F32)).astype(BF16)
    return a, b, c


def _expand_heads(a, c0):
    rows = a.shape[0]
    lane = lax.broadcasted_iota(jnp.int32, (rows, 128), 1)
    parts = []
    for p in range(GDN_HEADS // 2):
        lo = jnp.broadcast_to(a[:, c0 + 2 * p:c0 + 2 * p + 1], (rows, 128))
        hi = jnp.broadcast_to(a[:, c0 + 2 * p + 1:c0 + 2 * p + 2], (rows, 128))
        parts.append(jnp.where(lane < HEAD_DIM, lo, hi))
    return jnp.concatenate(parts, axis=1)


def _gdn_local_kernel(x_ref, xp_ref, xn_ref, gt_ref, cw_ref, av_ref, dt_ref, cos_ref, sin_ref,
                      ones_ref, bd_ref, lvl_ref, lvlbd_ref,
                      u_ref, w_ref, qg_ref, qk_ref, kdt_ref, eg_ref,
                      q_s, k_s, v_s, gc_s, be_s):
    s = pl.program_id(1)
    ns = pl.num_programs(1)
    width = x_ref.shape[-1]

    x = x_ref[0].astype(F32)
    prev_ok = (s >= 2).astype(F32)
    next_ok = jnp.logical_and(s >= 1, s < ns - 1).astype(F32)
    prev_row = xp_ref[0, HALO - 1:HALO, :].astype(F32) * prev_ok
    next_row = xn_ref[0, 0:1, :].astype(F32) * next_ok
    ri = lax.broadcasted_iota(jnp.int32, (TILE, width), 0)
    x_m1 = jnp.where(ri == 0, prev_row, pltpu.roll(x, 1, 0))
    x_p1 = jnp.where(ri == TILE - 1, next_row, pltpu.roll(x, TILE - 1, 0))
    y = cw_ref[0:1, :] * x_m1 + cw_ref[1:2, :] * x + cw_ref[2:3, :] * x_p1
    a = _silu(y)
    q = a[:, 0:GDN_W]
    k = a[:, GDN_W:2 * GDN_W]
    v_s[...] = a[:, 2 * GDN_W:3 * GDN_W]

    ones_bd = ones_ref[...]

    def seg_sum(t):
        hi = t.astype(BF16)
        lo = (t - hi.astype(F32)).astype(BF16)
        return _dot(hi, ones_bd) + _dot(lo, ones_bd)

    q = q * lax.rsqrt(seg_sum(q * q) + RMS_EPS)
    k = k * lax.rsqrt(seg_sum(k * k) + RMS_EPS)
    cos_t = cos_ref[...]
    sin_t = sin_ref[...]
    lane = lax.broadcasted_iota(jnp.int32, (TILE, 128), 1)
    first_half = (lane % HEAD_DIM) < HEAD_DIM // 2

    def rope(t):
        parts = []
        for p in range(GDN_W // 128):
            ts = t[:, 128 * p:128 * (p + 1)]
            partner = jnp.where(first_half, pltpu.roll(ts, 128 - HEAD_DIM // 2, 1),
                                pltpu.roll(ts, HEAD_DIM // 2, 1))
            parts.append(ts * cos_t + partner * sin_t)
        return jnp.concatenate(parts, axis=1)

    q_s[...] = rope(q) * HEAD_DIM ** -0.5
    k_s[...] = rope(k)

    gt = gt_ref[0]
    beta = _sigmoid(gt)
    g_raw = -jnp.exp(av_ref[...]) * _softplus(gt + dt_ref[...])
    r_i = lax.broadcasted_iota(jnp.int32, (TILE, TILE), 0)
    c_i = lax.broadcasted_iota(jnp.int32, (TILE, TILE), 1)
    same_chunk = (r_i // CHUNK) == (c_i // CHUNK)
    tri_f = jnp.where(jnp.logical_and(same_chunk, r_i >= c_i), 1.0, 0.0).astype(BF16)
    tri_b = jnp.where(jnp.logical_and(same_chunk, r_i <= c_i), 1.0, 0.0).astype(BF16)
    g1, g2, g3 = _split3(g_raw)
    cum_f = _dot(tri_f, g1) + _dot(tri_f, g2) + _dot(tri_f, g3)
    cum_b = _dot(tri_b, g1) + _dot(tri_b, g2) + _dot(tri_b, g3)
    gc_s[0] = _expand_heads(cum_f, 2 * GDN_HEADS)
    gc_s[1] = _expand_heads(cum_b, 3 * GDN_HEADS)
    be_s[0] = _expand_heads(beta, 0)
    be_s[1] = _expand_heads(beta, GDN_HEADS)

    bdmask = bd_ref[...]
    eye = lvl_ref[N_LEVELS]
    ii = lax.broadcasted_iota(jnp.int32, (CHUNK, GROUP_W), 0)
    jj = lax.broadcasted_iota(jnp.int32, (CHUNK, GROUP_W), 1) % HEAD_DIM

    def setup(c, d, gi):
        rows = slice(c * CHUNK, (c + 1) * CHUNK)
        lanes = slice(gi * GROUP_W, (gi + 1) * GROUP_W)
        kk = k_s[rows, lanes]
        qq = q_s[rows, lanes]
        gc = gc_s[d, rows, lanes]
        be = be_s[d, rows, lanes]
        tri = (ii >= jj) if d == 0 else (ii <= jj)
        g_last = gc[CHUNK - 1:CHUNK, :] if d == 0 else gc[0:1, :]
        kb = kk * be
        e_gc = jnp.exp(gc)
        gc_t = jnp.sum(gc * eye, axis=0, keepdims=True)
        dec = jnp.exp(jnp.where(tri, gc - gc_t, MASK_NEG))
        qg_ref[d, 0, rows, lanes] = (qq * e_gc).astype(BF16)
        return dict(c=c, rows=rows, lanes=lanes, d=d, kk=kk, dec=dec, g_last=g_last,
                    lhs=jnp.concatenate([kb, qq], axis=0).astype(BF16),
                    vb=(v_s[rows, lanes] * be).astype(BF16), kbg=(kb * e_gc).astype(BF16),
                    kd=(kk * jnp.exp(g_last - gc)).astype(BF16))

    for c0 in range(0, CHUNKS_PER_TILE, LOCAL_PAR_CHUNKS):
        ps = [setup(c, d, gi) for c in range(c0, c0 + LOCAL_PAR_CHUNKS) for d in range(2)
              for gi in range(GDN_W // GROUP_W)]
        grams = [_dot_nt(p["lhs"], _block_diag(p["kk"].astype(BF16), bdmask)) for p in ps]
        ms = []
        for p, gram in zip(ps, grams):
            ms.append(gram[0:CHUNK] * p["dec"])
            qk_ref[p["d"], 0, p["rows"], p["lanes"]] = (gram[CHUNK:2 * CHUNK] * p["dec"]).astype(BF16)
        xs = [eye - m * lvl_ref[0] for m in ms]
        xb = [x.astype(BF16) for x in xs]
        mt = [jnp.concatenate([m.astype(BF16)] * HEADS_PER_GROUP, axis=0) for m in ms]
        for lv in range(1, N_LEVELS):
            ys = [_dot(x, m * lvlbd_ref[lv]) for x, m in zip(xb, mt)]
            zs = [_dot(y.astype(BF16), _block_diag(x, bdmask)) for y, x in zip(ys, xb)]
            xs = [x - z for x, z in zip(xs, zs)]
            xb = [x.astype(BF16) for x in xs]
        us = [_dot(x, _block_diag(p["vb"], bdmask)) for x, p in zip(xb, ps)]
        ws = [_dot(x, _block_diag(p["kbg"], bdmask)) for x, p in zip(xb, ps)]
        kdts = [_dot_nt(eye.astype(BF16), _block_diag(p["kd"], bdmask)) for p in ps]
        for p, u, w, kdt in zip(ps, us, ws, kdts):
            u_ref[p["d"], 0, p["rows"], p["lanes"]] = u.astype(BF16)
            w_ref[p["d"], 0, p["rows"], p["lanes"]] = w.astype(BF16)
            kdt_ref[p["d"], 0, p["rows"], p["lanes"]] = kdt.astype(BF16)
            eg_ref[p["d"], 0, 0, p["c"]:p["c"] + 1, p["lanes"]] = jnp.exp(p["g_last"])


def _gdn_local(gqkv, gates, conv_w, avec, dtvec, cos_t, sin_t, ones_bd, bdmask, lvl, lvlbd):
    nb, s_len, width = gqkv.shape
    ns = s_len // TILE
    hpt = TILE // HALO
    nh = s_len // HALO
    dir_spec = pl.BlockSpec((2, 1, TILE, GDN_W), lambda b, s: (0, b, s, 0))
    dir_shape = jax.ShapeDtypeStruct((2, nb, s_len, GDN_W), BF16)
    return pl.pallas_call(
        _gdn_local_kernel,
        grid=(nb, ns),
        in_specs=[pl.BlockSpec((1, TILE, width), lambda b, s: (b, s, 0)),
                  pl.BlockSpec((1, HALO, width), lambda b, s: (b, jnp.maximum(s * hpt - 1, 0), 0)),
                  pl.BlockSpec((1, HALO, width),
                               lambda b, s: (b, jnp.minimum((s + 1) * hpt, nh - 1), 0)),
                  pl.BlockSpec((1, TILE, GATE_PAD), lambda b, s: (b, s, 0)),
                  _const_spec(conv_w.shape),
                  _const_spec(avec.shape),
                  _const_spec(dtvec.shape),
                  pl.BlockSpec((TILE, 128), lambda b, s: (s, 0)),
                  pl.BlockSpec((TILE, 128), lambda b, s: (s, 0)),
                  _const_spec(ones_bd.shape),
                  _const_spec(bdmask.shape),
                  _const_spec(lvl.shape),
                  _const_spec(lvlbd.shape)],
        out_specs=[dir_spec] * 5 + [pl.BlockSpec((2, 1, 1, CHUNKS_PER_TILE, GDN_W),
                                                 lambda b, s: (0, b, s, 0, 0))],
        out_shape=[dir_shape] * 5 + [jax.ShapeDtypeStruct((2, nb, ns, CHUNKS_PER_TILE, GDN_W), F32)],
        scratch_shapes=[pltpu.VMEM((TILE, GDN_W), F32), pltpu.VMEM((TILE, GDN_W), F32),
                        pltpu.VMEM((TILE, GDN_W), F32), pltpu.VMEM((2, TILE, GDN_W), F32),
                        pltpu.VMEM((2, TILE, GDN_W), F32)],
        compiler_params=_cparams(2),
        name="gdn_local",
    )(gqkv, gqkv, gqkv, gates, conv_w, avec, dtvec, cos_t, sin_t, ones_bd, bdmask, lvl, lvlbd)


def _gdn_scan_kernel(uf, wf, qgf, qkf, kdf, egf, ub, wb, qgb, qkb, kdb, egb, bd_ref,
                     of_ref, ob_ref, state):
    j = pl.program_id(1)

    @pl.when(j == 0)
    def _():
        state[...] = jnp.zeros_like(state)

    bdmask = bd_ref[...]
    ins = ((uf, wf, qgf, qkf, kdf, egf, of_ref), (ub, wb, qgb, qkb, kdb, egb, ob_ref))
    n_groups = GDN_W // GROUP_W
    chains = [(bi, d, gi) for bi in range(SCAN_BATCH) for d in range(2) for gi in range(n_groups)]
    sts = [state[i] for i in range(len(chains))]
    for ci in range(CHUNKS_PER_TILE):
        sl = []
        for bi, d, gi in chains:
            c = ci if d == 0 else CHUNKS_PER_TILE - 1 - ci
            sl.append((c, slice(c * CHUNK, (c + 1) * CHUNK), slice(gi * GROUP_W, (gi + 1) * GROUP_W)))
        r1s = []
        for (bi, d, gi), (c, rows, lanes), st in zip(chains, sl, sts):
            w_r, qg_r = ins[d][1], ins[d][2]
            lhs1 = jnp.concatenate([w_r[0, bi, rows, lanes], qg_r[0, bi, rows, lanes]], axis=0)
            r1s.append(_dot(lhs1, _block_diag(st.astype(BF16), bdmask)))
        r2s = []
        for (bi, d, gi), (c, rows, lanes), r1 in zip(chains, sl, r1s):
            u_r, qk_r, kd_r = ins[d][0], ins[d][3], ins[d][4]
            v_new = u_r[0, bi, rows, lanes].astype(F32) - r1[0:CHUNK]
            lhs2 = jnp.concatenate([qk_r[0, bi, rows, lanes], kd_r[0, bi, rows, lanes]], axis=0)
            r2s.append(_dot(lhs2, _block_diag(v_new.astype(BF16), bdmask)))
        new_sts = []
        for (bi, d, gi), (c, rows, lanes), st, r1, r2 in zip(chains, sl, sts, r1s, r2s):
            eg_r, o_r = ins[d][5], ins[d][6]
            o_r[bi, rows, lanes] = (r1[CHUNK:2 * CHUNK] + r2[0:CHUNK]).astype(BF16)
            new_sts.append(st * eg_r[0, bi, 0, c:c + 1, lanes] + r2[CHUNK:2 * CHUNK])
        sts = new_sts
    for i, st in enumerate(sts):
        state[i] = st


def _gdn_scan(u, w, qg, qk, kdt, eg, bdmask):
    _, nb, s_len, _ = u.shape
    ns = s_len // TILE
    sb = SCAN_BATCH
    assert nb % sb == 0
    bwd = lambda j: jnp.where(j == 0, 0, ns - j)
    f_spec = pl.BlockSpec((1, sb, TILE, GDN_W), lambda b, j: (0, b, j, 0))
    b_spec = pl.BlockSpec((1, sb, TILE, GDN_W), lambda b, j: (1, b, bwd(j), 0))
    egf_spec = pl.BlockSpec((1, sb, 1, CHUNKS_PER_TILE, GDN_W), lambda b, j: (0, b, j, 0, 0))
    egb_spec = pl.BlockSpec((1, sb, 1, CHUNKS_PER_TILE, GDN_W), lambda b, j: (1, b, bwd(j), 0, 0))
    out_shape = jax.ShapeDtypeStruct((nb, s_len, GDN_W), BF16)
    return pl.pallas_call(
        _gdn_scan_kernel,
        grid=(nb // sb, ns),
        in_specs=[f_spec] * 5 + [egf_spec] + [b_spec] * 5 + [egb_spec] + [_const_spec(bdmask.shape)],
        out_specs=[pl.BlockSpec((sb, TILE, GDN_W), lambda b, j: (b, j, 0)),
                   pl.BlockSpec((sb, TILE, GDN_W), lambda b, j: (b, bwd(j), 0))],
        out_shape=[out_shape, out_shape],
        scratch_shapes=[pltpu.VMEM((sb * 2 * GDN_W // GROUP_W, CHUNK, GROUP_W), F32)],
        compiler_params=_cparams(2),
        name="gdn_scan",
    )(u, w, qg, qk, kdt, eg, u, w, qg, qk, kdt, eg, bdmask)


def _na_kernel(q_ref, k_ref, v_ref, bias_ref, o_ref, *, n_rows):
    s = pl.program_id(2)
    lane = lax.broadcasted_iota(jnp.int32, (1, 128), 1)
    head_mask = (lane < HEAD_DIM, lane >= HEAD_DIM)

    @pl.when(s == 0)
    def _():
        q = q_ref[0]
        kc = k_ref[0, 0:TILE, :]
        vc = v_ref[0, 0:TILE, :]
        outs = []
        for h in range(2):
            qh = jnp.where(head_mask[h], q, jnp.zeros_like(q))
            sc = _dot_nt(qh, kc)
            p = jnp.exp(sc - jnp.max(sc, axis=-1, keepdims=True))
            outs.append(_dot(p.astype(BF16), vc) / jnp.sum(p, axis=-1, keepdims=True))
        o_ref[0] = jnp.where(head_mask[0], outs[0], outs[1]).astype(BF16)

    @pl.when(s > 0)
    def _():
        kc = k_ref[0, 0:TILE, :]
        vc = v_ref[0, 0:TILE, :]
        rows_per_tile = TILE // GRID_W
        q2, kws, vws, biases = [], [], [], []
        for r in range(rows_per_tile):
            row = (s - 1) * rows_per_tile + r
            rs = jnp.clip(row - NA_KH // 2, 0, n_rows - NA_KH)
            cls = rs - row + (NA_KH - 1)
            start = pl.multiple_of(TILE + rs * GRID_W, GRID_W)
            kws.append(k_ref[0, pl.ds(start, NA_KH * GRID_W), :])
            vws.append(v_ref[0, pl.ds(start, NA_KH * GRID_W), :])
            biases.append(jnp.concatenate([bias_ref[0, cls], bias_ref[1, cls]], axis=0))
            q = q_ref[0, r * GRID_W:(r + 1) * GRID_W, :]
            q2.append(jnp.concatenate([jnp.where(head_mask[h], q, jnp.zeros_like(q))
                                       for h in range(2)], axis=0))
        sc_all = _dot_nt(jnp.concatenate(q2, axis=0), kc)
        sws = [_dot_nt(q2[r], kws[r]) + biases[r] for r in range(rows_per_tile)]
        pws, pcs, dens = [], [], []
        for r in range(rows_per_tile):
            sc = sc_all[2 * GRID_W * r:2 * GRID_W * (r + 1)]
            mx = jnp.maximum(jnp.max(sws[r], axis=-1, keepdims=True),
                             jnp.max(sc, axis=-1, keepdims=True))
            pw = jnp.exp(sws[r] - mx)
            pc = jnp.exp(sc - mx)
            dens.append(jnp.sum(pw, axis=-1, keepdims=True) + jnp.sum(pc, axis=-1, keepdims=True))
            pws.append(pw.astype(BF16))
            pcs.append(pc.astype(BF16))
        oc_all = _dot(jnp.concatenate(pcs, axis=0), vc)
        results = []
        for r in range(rows_per_tile):
            o2 = (_dot(pws[r], vws[r]) + oc_all[2 * GRID_W * r:2 * GRID_W * (r + 1)]) / dens[r]
            results.append(jnp.where(head_mask[0], o2[0:GRID_W], o2[GRID_W:2 * GRID_W]).astype(BF16))
        o_ref[0] = jnp.concatenate(results, axis=0)


def _na_attention(q, k, v, bias_tab):
    nb, s_len, _ = q.shape
    ns = s_len // TILE
    n_rows = (s_len - TILE) // GRID_W
    return pl.pallas_call(
        functools.partial(_na_kernel, n_rows=n_rows),
        grid=(nb, NA_W // 128, ns),
        in_specs=[pl.BlockSpec((1, TILE, 128), lambda b, hp, s: (b, s, hp)),
                  pl.BlockSpec((1, s_len, 128), lambda b, hp, s: (b, 0, hp)),
                  pl.BlockSpec((1, s_len, 128), lambda b, hp, s: (b, 0, hp)),
                  pl.BlockSpec((2, NA_KH, GRID_W, NA_KH * GRID_W), lambda b, hp, s: (hp, 0, 0, 0))],
        out_specs=pl.BlockSpec((1, TILE, 128), lambda b, hp, s: (b, s, hp)),
        out_shape=jax.ShapeDtypeStruct((nb, s_len, NA_W), BF16),
        compiler_params=_cparams(3),
        name="na_attn",
    )(q, k, v, bias_tab)


def _na_bias_tables(rpb):
    n_l, n_h, n_dr, n_dc = rpb.shape
    period = 2 * GRID_W
    left = (GRID_W - 1) - (NA_KW - 1)
    e = jnp.pad(rpb.astype(F32), ((0, 0), (0, 0), (0, 0), (left, period - n_dc - left)))
    skew = jnp.tile(e, (1, 1, 1, GRID_W))[..., :GRID_W * (period - 1)]
    skew = skew.reshape(n_l, n_h, n_dr, GRID_W, period - 1)[..., GRID_W - 1:]
    qi = np.arange(GRID_W)[:, None]
    kc = np.arange(GRID_W)[None, :]
    cs = np.clip(qi - NA_KW // 2, 0, GRID_W - NA_KW)
    in_win = np.logical_and(kc >= cs, kc < cs + NA_KW)
    toe = jnp.where(in_win, skew, MASK_NEG)
    toe = jnp.transpose(toe, (0, 1, 3, 2, 4))
    tab = jnp.stack([toe[:, :, :, c:c + NA_KH, :] for c in range(NA_KH)], axis=2)
    return tab.reshape(n_l, n_h, NA_KH, GRID_W, NA_KH * GRID_W)


def _post_mixer_kernel(x_ref, mod_ref, ona_ref, of_ref, ob_ref, z_ref, gg_ref, gp_ref, w_ref,
                       ones_ref, o_ref):
    d = x_ref.shape[-1]
    og = of_ref[0].astype(F32) + ob_ref[0].astype(F32)
    sq = og * og
    hi = sq.astype(BF16)
    lo = (sq - hi.astype(F32)).astype(BF16)
    ms = (_dot(hi, ones_ref[...]) + _dot(lo, ones_ref[...])) * (1.0 / HEAD_DIM)
    gated = og * lax.rsqrt(ms + RMS_EPS) * gg_ref[...] * _silu(z_ref[0].astype(F32))
    y = _dot(ona_ref[0], w_ref[0:NA_W, :]) + _dot(gated.astype(BF16), w_ref[NA_W:NA_W + GDN_W, :])
    gate = mod_ref[0][:, 2 * d:3 * d]
    o_ref[0] = x_ref[0] + gate * _rms(y, gp_ref[...])


def _post_mixer(x_all, mod_l, o_na, o_f, o_b, z, gdn_gain, post_gain, w_out, ones_bd):
    nb, s_len, d = x_all.shape
    ns = s_len // TILE
    tok = lambda w: pl.BlockSpec((1, TILE, w), lambda b, s: (b, s, 0))
    return pl.pallas_call(
        _post_mixer_kernel,
        grid=(nb, ns),
        in_specs=[tok(d),
                  pl.BlockSpec((1, 1, N_MOD * d), lambda b, s: (_mod_row(b, s, nb), 0, 0)),
                  tok(NA_W), tok(GDN_W), tok(GDN_W), tok(GDN_W),
                  _const_spec(gdn_gain.shape), _const_spec(post_gain.shape),
                  _const_spec(w_out.shape), _const_spec(ones_bd.shape)],
        out_specs=tok(d),
        out_shape=jax.ShapeDtypeStruct(x_all.shape, F32),
        input_output_aliases={0: 0},
        compiler_params=_cparams(2),
        name="post_mixer",
    )(x_all, mod_l, o_na, o_f, o_b, z, gdn_gain, post_gain, w_out, ones_bd)


def _conv_ffn_kernel(x_ref, xp_ref, xn_ref, mod_ref, gpre_ref, gpost_ref, wu_ref, cw_ref, wd_ref,
                     o_ref, act_s, *, first_block):
    s = pl.program_id(1) + first_block
    ns = pl.num_programs(1) + first_block
    d = x_ref.shape[-1]
    d_ff = wd_ref.shape[0]
    m = mod_ref[0]
    shift, scale, gate = m[:, 3 * d:4 * d], m[:, 4 * d:5 * d], m[:, 5 * d:6 * d]
    x = x_ref[0]
    prev_ok = (s >= 2).astype(F32)
    next_ok = jnp.logical_and(s >= 1, s < ns - 1).astype(F32)

    def hidden(t):
        return _rms(t, gpre_ref[...]) * (1.0 + scale) + shift

    h = jnp.concatenate([hidden(xp_ref[0]) * prev_ok, hidden(x), hidden(xn_ref[0]) * next_ok],
                        axis=0).astype(BF16)
    n_steps = d_ff // FFN_COLS

    def up(c):
        return [_dot(h, wu_ref[:, off:off + FFN_COLS]) for off in (c * FFN_COLS, d_ff + c * FFN_COLS)]

    def activation(c, us):
        halves = []
        n_rows = TILE + 2 * HALO
        for off, u in zip((c * FFN_COLS, d_ff + c * FFN_COLS), us):
            u_m1 = pltpu.roll(u, 1, 0)[HALO:HALO + TILE]
            u_p1 = pltpu.roll(u, n_rows - 1, 0)[HALO:HALO + TILE]
            halves.append(cw_ref[0:1, off:off + FFN_COLS] * u_m1
                          + cw_ref[1:2, off:off + FFN_COLS] * u[HALO:HALO + TILE]
                          + cw_ref[2:3, off:off + FFN_COLS] * u_p1)
        return (_silu(halves[0]) * halves[1]).astype(BF16)

    u_next = up(0)
    for c in range(n_steps):
        u_cur = u_next
        if c + 1 < n_steps:
            u_next = up(c + 1)
        act_s[:, c * FFN_COLS:(c + 1) * FFN_COLS] = activation(c, u_cur)
    acc = _dot(act_s[...], wd_ref[...])
    o_ref[0] = x + gate * _rms(acc, gpost_ref[...])


def _conv_ffn(x_all, mod_l, pre_gain, post_gain, w_up, conv_w, w_down, latent_only):
    nb, s_len, d = x_all.shape
    ns = s_len // TILE
    hpt = TILE // HALO
    nh = s_len // HALO
    first = 1 if latent_only else 0
    return pl.pallas_call(
        functools.partial(_conv_ffn_kernel, first_block=first),
        grid=(nb, ns - first),
        in_specs=[pl.BlockSpec((1, TILE, d), lambda b, s: (b, s + first, 0)),
                  pl.BlockSpec((1, HALO, d),
                               lambda b, s: (b, jnp.maximum((s + first) * hpt - 1, 0), 0)),
                  pl.BlockSpec((1, HALO, d),
                               lambda b, s: (b, jnp.minimum((s + first + 1) * hpt, nh - 1), 0)),
                  pl.BlockSpec((1, 1, N_MOD * d),
                               lambda b, s: (_mod_row(b, s + first, nb), 0, 0)),
                  _const_spec(pre_gain.shape), _const_spec(post_gain.shape),
                  _const_spec(w_up.shape), _const_spec(conv_w.shape), _const_spec(w_down.shape)],
        out_specs=pl.BlockSpec((1, TILE, d), lambda b, s: (b, s, 0)),
        out_shape=jax.ShapeDtypeStruct((nb, s_len - first * TILE, d), F32),
        scratch_shapes=[pltpu.VMEM((TILE, w_down.shape[0]), BF16)],
        compiler_params=_cparams(2),
        name="conv_ffn",
    )(x_all, x_all, x_all, mod_l, pre_gain, post_gain, w_up, conv_w, w_down)


def _rotary_tables(n_ctx, n_lat):
    t = jnp.arange(n_lat)
    row = (t // GRID_W).astype(F32)
    col = (t % GRID_W).astype(F32)
    pairs = HEAD_DIM // 4
    inv_freq = ROPE_BASE ** (-jnp.arange(pairs, dtype=F32) / pairs)
    ang = jnp.concatenate([row[:, None] * inv_freq, col[:, None] * inv_freq], axis=-1)
    cos, sin = jnp.cos(ang), jnp.sin(ang)
    cos64 = jnp.concatenate([cos, cos], axis=-1)
    sin64 = jnp.concatenate([-sin, sin], axis=-1)
    cos_t = jnp.concatenate([jnp.ones((n_ctx, HEAD_DIM), F32), cos64], axis=0)
    sin_t = jnp.concatenate([jnp.zeros((n_ctx, HEAD_DIM), F32), sin64], axis=0)
    return jnp.tile(cos_t, (1, 2)), jnp.tile(sin_t, (1, 2))


def _static_masks():
    lane_h = np.arange(GROUP_W) // HEAD_DIM
    bdmask = (lane_h[:, None] == lane_h[None, :]).astype(np.float32)
    seg = np.arange(GDN_W) // HEAD_DIM
    ones_bd = (seg[:, None] == seg[None, :]).astype(np.float32)
    i = np.arange(CHUNK)[:, None]
    j = (np.arange(GROUP_W) % HEAD_DIM)[None, :]
    lvl = []
    for lv in range(N_LEVELS):
        n = 2 ** lv
        lvl.append(np.logical_and(i // (2 * n) == j // (2 * n), i // n != j // n))
    lvl.append(i == j)
    lvl = np.stack(lvl).astype(np.float32)
    lvlbd = np.tile(lvl[:N_LEVELS], (1, HEADS_PER_GROUP, 1)) * bdmask[None]
    return (jnp.asarray(bdmask, BF16), jnp.asarray(ones_bd, BF16), jnp.asarray(lvl, F32),
            jnp.asarray(lvlbd, BF16))


def _gate_lane_vector(p):
    flat = p.reshape(1, 2 * GDN_HEADS).astype(F32)
    return jnp.pad(flat, ((0, 0), (2 * GDN_HEADS, GATE_PAD - 4 * GDN_HEADS)))


def kernel(x, c, ctx, c_ctx, ada_w, ada_b, norm_mix_pre, norm_mix_post, w_in, qkv_conv, a_log,
           dt_bias, gdn_norm, rpb, w_out, norm_ffn_pre, norm_ffn_post, ffn_up, ffn_conv, ffn_down):
    nb, n_lat, d = x.shape
    n_ctx = ctx.shape[1]
    depth = ada_w.shape[0]
    assert n_ctx == TILE and n_lat % TILE == 0 and n_lat // GRID_W >= NA_KH

    mod_rows = -(-(nb + 1) // 8) * 8
    cc = jnp.concatenate([c, c_ctx[None, :], jnp.zeros((mod_rows - nb - 1, d), F32)], axis=0)
    mod_all = _modulation(cc, ada_w, ada_b)

    cos_t, sin_t = _rotary_tables(n_ctx, n_lat)
    bdmask, ones_bd, lvl, lvlbd = _static_masks()
    in_w = w_in.shape[-1]
    w_in_p = jnp.pad(w_in, ((0, 0), (0, 0), (0, GATE_PAD - (in_w - 3 * NA_W - 4 * GDN_W)))).astype(BF16)
    w_out_b = w_out.astype(BF16)
    w_up_b = ffn_up.astype(BF16)
    w_down_b = ffn_down.astype(BF16)
    bias_tabs = _na_bias_tables(rpb)

    x_all = jnp.concatenate([ctx, x], axis=1)
    for l in range(depth):
        mod_l = mod_all[l].reshape(mod_rows, 1, N_MOD * d)
        naq, nak, nav, gqkv, z, gates = _pre_mixer(x_all, mod_l, norm_mix_pre[l][None, :], w_in_p[l])
        u, w, qg, qk, kdt, eg = _gdn_local(gqkv, gates, qkv_conv[l], _gate_lane_vector(a_log[l]),
                                           _gate_lane_vector(dt_bias[l]), cos_t, sin_t, ones_bd,
                                           bdmask, lvl, lvlbd)
        o_f, o_b = _gdn_scan(u, w, qg, qk, kdt, eg, bdmask)
        o_na = _na_attention(naq, nak, nav, bias_tabs[l])
        x_all = _post_mixer(x_all, mod_l, o_na, o_f, o_b, z, jnp.tile(gdn_norm[l], GDN_HEADS)[None, :],
                            norm_mix_post[l][None, :], w_out_b[l], ones_bd)
        x_all = _conv_ffn(x_all, mod_l, norm_ffn_pre[l][None, :], norm_ffn_post[l][None, :],
                          w_up_b[l], ffn_conv[l], w_down_b[l], latent_only=(l == depth - 1))
    return x_all
```

```python
import functools

import jax
import jax.numpy as jnp
import numpy as np
from jax import lax
from jax.experimental import pallas as pl
from jax.experimental.pallas import tpu as pltpu

F32 = jnp.float32
BF16 = jnp.bfloat16

HEAD_DIM = 64
NA_HEADS = 8
GDN_HEADS = 8
NA_W = NA_HEADS * HEAD_DIM
GDN_W = GDN_HEADS * HEAD_DIM
GRID_W = 64
NA_KH = 8
NA_KW = 16
CHUNK = 64
ROPE_BASE = 10000.0
RMS_EPS = 1e-6
N_MOD = 6
TILE = 256
CHUNKS_PER_TILE = TILE // CHUNK
HALO = 8
GROUP_W = 128
HEADS_PER_GROUP = GROUP_W // HEAD_DIM
N_LEVELS = 6
LOCAL_PAR_CHUNKS = 2
NA_PAIRS = 2
POST_ROWS = 128
SCAN_BATCH = 2
MASK_NEG = -1e30
GATE_PAD = 128
FFN_COLS = 256
VMEM_LIMIT = 56 * 1024 * 1024


def _cparams(n_axes):
    return pltpu.CompilerParams(dimension_semantics=("arbitrary",) * n_axes,
                                vmem_limit_bytes=VMEM_LIMIT)


def _const_spec(shape):
    nd = len(shape)
    return pl.BlockSpec(shape, lambda *_: (0,) * nd)


def _sigmoid(x):
    return 1.0 / (1.0 + jnp.exp(-x))


def _silu(x):
    return x * _sigmoid(x)


def _softplus(x):
    return jnp.maximum(x, 0.0) + jnp.log(1.0 + jnp.exp(-jnp.abs(x)))


def _rms(x, gain):
    ms = jnp.mean(x * x, axis=-1, keepdims=True)
    return x * lax.rsqrt(ms + RMS_EPS) * gain


def _dot(a, b):
    return jnp.dot(a, b, preferred_element_type=F32)


def _dot_nt(a, b):
    return lax.dot_general(a, b, (((1,), (1,)), ((), ())), preferred_element_type=F32)


def _mod_row(b, s, n_batch):
    return jnp.where(s == 0, n_batch, b)


def _modulation_kernel(c_ref, w_ref, b_ref, o_ref):
    a = _silu(c_ref[...]).astype(BF16)
    o_ref[0] = _dot(a, w_ref[0].astype(BF16)) + b_ref[0]


def _modulation(cc, ada_w, ada_b):
    depth, d, n = ada_w.shape
    rows = cc.shape[0]
    nblk = n // N_MOD
    return pl.pallas_call(
        _modulation_kernel,
        grid=(depth, N_MOD),
        in_specs=[pl.BlockSpec((rows, d), lambda l, j: (0, 0)),
                  pl.BlockSpec((1, d, nblk), lambda l, j: (l, 0, j)),
                  pl.BlockSpec((1, 1, nblk), lambda l, j: (l, 0, j))],
        out_specs=pl.BlockSpec((1, rows, nblk), lambda l, j: (l, 0, j)),
        out_shape=jax.ShapeDtypeStruct((depth, rows, n), F32),
        compiler_params=_cparams(2),
        name="modulation",
    )(cc, ada_w, ada_b.reshape(depth, 1, n))


def _pre_mixer_kernel(x_ref, mod_ref, g_ref, w_ref, q_ref, k_ref, v_ref, gq_ref, z_ref, gt_ref):
    d = x_ref.shape[-1]
    m = mod_ref[0]
    h = _rms(x_ref[0], g_ref[...]) * (1.0 + m[:, d:2 * d]) + m[:, 0:d]
    hb = h.astype(BF16)
    o = 0
    q_ref[0] = (_dot(hb, w_ref[:, o:o + NA_W]) * HEAD_DIM ** -0.5).astype(BF16)
    o += NA_W
    k_ref[0] = _dot(hb, w_ref[:, o:o + NA_W]).astype(BF16)
    o += NA_W
    v_ref[0] = _dot(hb, w_ref[:, o:o + NA_W]).astype(BF16)
    o += NA_W
    gq_ref[0] = _dot(hb, w_ref[:, o:o + 3 * GDN_W]).astype(BF16)
    o += 3 * GDN_W
    z_ref[0] = _dot(hb, w_ref[:, o:o + GDN_W]).astype(BF16)
    o += GDN_W
    gt_ref[0] = _dot(hb, w_ref[:, o:o + GATE_PAD])


def _pre_mixer(x_all, mod_l, gain, w_in_p):
    nb, s_len, d = x_all.shape
    ns = s_len // TILE
    tok = lambda w: pl.BlockSpec((1, TILE, w), lambda b, s: (b, s, 0))
    shp = lambda w, dt: jax.ShapeDtypeStruct((nb, s_len, w), dt)
    return pl.pallas_call(
        _pre_mixer_kernel,
        grid=(nb, ns),
        in_specs=[tok(d),
                  pl.BlockSpec((1, 1, N_MOD * d), lambda b, s: (_mod_row(b, s, nb), 0, 0)),
                  _const_spec((1, d)),
                  _const_spec(w_in_p.shape)],
        out_specs=[tok(NA_W), tok(NA_W), tok(NA_W), tok(3 * GDN_W), tok(GDN_W), tok(GATE_PAD)],
        out_shape=[shp(NA_W, BF16), shp(NA_W, BF16), shp(NA_W, BF16), shp(3 * GDN_W, BF16),
                   shp(GDN_W, BF16), shp(GATE_PAD, F32)],
        compiler_params=_cparams(2),
        name="pre_mixer",
    )(x_all, mod_l, gain, w_in_p)


def _block_diag(x, bdmask):
    return jnp.concatenate([x] * HEADS_PER_GROUP, axis=0) * bdmask


def _split3(x):
    a = x.astype(BF16)
    r = x - a.astype(F32)
    b = r.astype(BF16)
    c = (r - b.astype(F32)).astype(BF16)
    return a, b, c


def _expand_heads(a, c0):
    rows = a.shape[0]
    lane = lax.broadcasted_iota(jnp.int32, (rows, 128), 1)
    parts = []
    for p in range(GDN_HEADS // 2):
        lo = jnp.broadcast_to(a[:, c0 + 2 * p:c0 + 2 * p + 1], (rows, 128))
        hi = jnp.broadcast_to(a[:, c0 + 2 * p + 1:c0 + 2 * p + 2], (rows, 128))
        parts.append(jnp.where(lane < HEAD_DIM, lo, hi))
    return jnp.concatenate(parts, axis=1)


def _gdn_local_kernel(x_ref, xp_ref, xn_ref, gt_ref, cw_ref, av_ref, dt_ref, cos_ref, sin_ref,
                      ones_ref, bd_ref, lvl_ref, lvlbd_ref,
                      u_ref, w_ref, qg_ref, qk_ref, kdt_ref, eg_ref,
                      q_s, k_s, v_s, gc_s, be_s):
    s = pl.program_id(1)
    ns = pl.num_programs(1)
    width = x_ref.shape[-1]

    x = x_ref[0].astype(F32)
    prev_ok = (s >= 2).astype(F32)
    next_ok = jnp.logical_and(s >= 1, s < ns - 1).astype(F32)
    prev_row = xp_ref[0, HALO - 1:HALO, :].astype(F32) * prev_ok
    next_row = xn_ref[0, 0:1, :].astype(F32) * next_ok
    ri = lax.broadcasted_iota(jnp.int32, (TILE, width), 0)
    x_m1 = jnp.where(ri == 0, prev_row, pltpu.roll(x, 1, 0))
    x_p1 = jnp.where(ri == TILE - 1, next_row, pltpu.roll(x, TILE - 1, 0))
    y = cw_ref[0:1, :] * x_m1 + cw_ref[1:2, :] * x + cw_ref[2:3, :] * x_p1
    a = _silu(y)
    q = a[:, 0:GDN_W]
    k = a[:, GDN_W:2 * GDN_W]
    v_s[...] = a[:, 2 * GDN_W:3 * GDN_W]

    ones_bd = ones_ref[...]

    def seg_sum(t):
        hi = t.astype(BF16)
        lo = (t - hi.astype(F32)).astype(BF16)
        return _dot(hi, ones_bd) + _dot(lo, ones_bd)

    q = q * lax.rsqrt(seg_sum(q * q) + RMS_EPS)
    k = k * lax.rsqrt(seg_sum(k * k) + RMS_EPS)
    cos_t = cos_ref[...]
    sin_t = sin_ref[...]
    lane = lax.broadcasted_iota(jnp.int32, (TILE, 128), 1)
    first_half = (lane % HEAD_DIM) < HEAD_DIM // 2

    def rope(t):
        parts = []
        for p in range(GDN_W // 128):
            ts = t[:, 128 * p:128 * (p + 1)]
            partner = jnp.where(first_half, pltpu.roll(ts, 128 - HEAD_DIM // 2, 1),
                                pltpu.roll(ts, HEAD_DIM // 2, 1))
            parts.append(ts * cos_t + partner * sin_t)
        return jnp.concatenate(parts, axis=1)

    q_s[...] = rope(q) * HEAD_DIM ** -0.5
    k_s[...] = rope(k)

    gt = gt_ref[0]
    beta = _sigmoid(gt)
    g_raw = -jnp.exp(av_ref[...]) * _softplus(gt + dt_ref[...])
    r_i = lax.broadcasted_iota(jnp.int32, (TILE, TILE), 0)
    c_i = lax.broadcasted_iota(jnp.int32, (TILE, TILE), 1)
    same_chunk = (r_i // CHUNK) == (c_i // CHUNK)
    tri_f = jnp.where(jnp.logical_and(same_chunk, r_i >= c_i), 1.0, 0.0).astype(BF16)
    tri_b = jnp.where(jnp.logical_and(same_chunk, r_i <= c_i), 1.0, 0.0).astype(BF16)
    g1, g2, g3 = _split3(g_raw)
    cum_f = _dot(tri_f, g1) + _dot(tri_f, g2) + _dot(tri_f, g3)
    cum_b = _dot(tri_b, g1) + _dot(tri_b, g2) + _dot(tri_b, g3)
    gc_s[0] = _expand_heads(cum_f, 2 * GDN_HEADS)
    gc_s[1] = _expand_heads(cum_b, 3 * GDN_HEADS)
    be_s[0] = _expand_heads(beta, 0)
    be_s[1] = _expand_heads(beta, GDN_HEADS)

    bdmask = bd_ref[...]
    eye = lvl_ref[N_LEVELS]
    ii = lax.broadcasted_iota(jnp.int32, (CHUNK, GROUP_W), 0)
    jj = lax.broadcasted_iota(jnp.int32, (CHUNK, GROUP_W), 1) % HEAD_DIM

    def setup(c, d, gi):
        rows = slice(c * CHUNK, (c + 1) * CHUNK)
        lanes = slice(gi * GROUP_W, (gi + 1) * GROUP_W)
        kk = k_s[rows, lanes]
        qq = q_s[rows, lanes]
        gc = gc_s[d, rows, lanes]
        be = be_s[d, rows, lanes]
        tri = (ii >= jj) if d == 0 else (ii <= jj)
        g_last = gc[CHUNK - 1:CHUNK, :] if d == 0 else gc[0:1, :]
        kb = kk * be
        e_gc = jnp.exp(gc)
        gc_t = jnp.sum(gc * eye, axis=0, keepdims=True)
        dec = jnp.exp(jnp.where(tri, gc - gc_t, MASK_NEG))
        qg_ref[d, 0, rows, lanes] = (qq * e_gc).astype(BF16)
        return dict(c=c, rows=rows, lanes=lanes, d=d, kk=kk, dec=dec, g_last=g_last,
                    lhs=jnp.concatenate([kb, qq], axis=0).astype(BF16),
                    vb=(v_s[rows, lanes] * be).astype(BF16), kbg=(kb * e_gc).astype(BF16),
                    kd=(kk * jnp.exp(g_last - gc)).astype(BF16))

    for c0 in range(0, CHUNKS_PER_TILE, LOCAL_PAR_CHUNKS):
        ps = [setup(c, d, gi) for c in range(c0, c0 + LOCAL_PAR_CHUNKS) for d in range(2)
              for gi in range(GDN_W // GROUP_W)]
        grams = [_dot_nt(p["lhs"], _block_diag(p["kk"].astype(BF16), bdmask)) for p in ps]
        ms = []
        for p, gram in zip(ps, grams):
            ms.append(gram[0:CHUNK] * p["dec"])
            qk_ref[p["d"], 0, p["rows"], p["lanes"]] = (gram[CHUNK:2 * CHUNK] * p["dec"]).astype(BF16)
        xs = [eye - m * lvl_ref[0] for m in ms]
        xb = [x.astype(BF16) for x in xs]
        mt = [jnp.concatenate([m.astype(BF16)] * HEADS_PER_GROUP, axis=0) for m in ms]
        for lv in range(1, N_LEVELS):
            ys = [_dot(x, m * lvlbd_ref[lv]) for x, m in zip(xb, mt)]
            zs = [_dot(y.astype(BF16), _block_diag(x, bdmask)) for y, x in zip(ys, xb)]
            xs = [x - z for x, z in zip(xs, zs)]
            xb = [x.astype(BF16) for x in xs]
        uws = [_dot(x, jnp.concatenate([_block_diag(p["vb"], bdmask), _block_diag(p["kbg"], bdmask)],
                                       axis=1)) for x, p in zip(xb, ps)]
        us = [uw[:, 0:GROUP_W] for uw in uws]
        ws = [uw[:, GROUP_W:2 * GROUP_W] for uw in uws]
        kdts = [_dot_nt(eye.astype(BF16), _block_diag(p["kd"], bdmask)) for p in ps]
        for p, u, w, kdt in zip(ps, us, ws, kdts):
            u_ref[p["d"], 0, p["rows"], p["lanes"]] = u.astype(BF16)
            w_ref[p["d"], 0, p["rows"], p["lanes"]] = w.astype(BF16)
            kdt_ref[p["d"], 0, p["rows"], p["lanes"]] = kdt.astype(BF16)
            eg_ref[p["d"], 0, 0, p["c"]:p["c"] + 1, p["lanes"]] = jnp.exp(p["g_last"])


def _gdn_local(gqkv, gates, conv_w, avec, dtvec, cos_t, sin_t, ones_bd, bdmask, lvl, lvlbd):
    nb, s_len, width = gqkv.shape
    ns = s_len // TILE
    hpt = TILE // HALO
    nh = s_len // HALO
    dir_spec = pl.BlockSpec((2, 1, TILE, GDN_W), lambda b, s: (0, b, s, 0))
    dir_shape = jax.ShapeDtypeStruct((2, nb, s_len, GDN_W), BF16)
    return pl.pallas_call(
        _gdn_local_kernel,
        grid=(nb, ns),
        in_specs=[pl.BlockSpec((1, TILE, width), lambda b, s: (b, s, 0)),
                  pl.BlockSpec((1, HALO, width), lambda b, s: (b, jnp.maximum(s * hpt - 1, 0), 0)),
                  pl.BlockSpec((1, HALO, width),
                               lambda b, s: (b, jnp.minimum((s + 1) * hpt, nh - 1), 0)),
                  pl.BlockSpec((1, TILE, GATE_PAD), lambda b, s: (b, s, 0)),
                  _const_spec(conv_w.shape),
                  _const_spec(avec.shape),
                  _const_spec(dtvec.shape),
                  pl.BlockSpec((TILE, 128), lambda b, s: (s, 0)),
                  pl.BlockSpec((TILE, 128), lambda b, s: (s, 0)),
                  _const_spec(ones_bd.shape),
                  _const_spec(bdmask.shape),
                  _const_spec(lvl.shape),
                  _const_spec(lvlbd.shape)],
        out_specs=[dir_spec] * 5 + [pl.BlockSpec((2, 1, 1, CHUNKS_PER_TILE, GDN_W),
                                                 lambda b, s: (0, b, s, 0, 0))],
        out_shape=[dir_shape] * 5 + [jax.ShapeDtypeStruct((2, nb, ns, CHUNKS_PER_TILE, GDN_W), F32)],
        scratch_shapes=[pltpu.VMEM((TILE, GDN_W), F32), pltpu.VMEM((TILE, GDN_W), F32),
                        pltpu.VMEM((TILE, GDN_W), F32), pltpu.VMEM((2, TILE, GDN_W), F32),
                        pltpu.VMEM((2, TILE, GDN_W), F32)],
        compiler_params=_cparams(2),
        name="gdn_local",
    )(gqkv, gqkv, gqkv, gates, conv_w, avec, dtvec, cos_t, sin_t, ones_bd, bdmask, lvl, lvlbd)


def _gdn_scan_kernel(uf, wf, qgf, qkf, kdf, egf, ub, wb, qgb, qkb, kdb, egb, bd_ref,
                     of_ref, ob_ref, state):
    j = pl.program_id(1)

    @pl.when(j == 0)
    def _():
        state[...] = jnp.zeros_like(state)

    bdmask = bd_ref[...]
    ins = ((uf, wf, qgf, qkf, kdf, egf, of_ref), (ub, wb, qgb, qkb, kdb, egb, ob_ref))
    n_groups = GDN_W // GROUP_W
    chains = [(bi, d, gi) for bi in range(SCAN_BATCH) for d in range(2) for gi in range(n_groups)]
    sts = [state[i] for i in range(len(chains))]
    for ci in range(CHUNKS_PER_TILE):
        sl = []
        for bi, d, gi in chains:
            c = ci if d == 0 else CHUNKS_PER_TILE - 1 - ci
            sl.append((c, slice(c * CHUNK, (c + 1) * CHUNK), slice(gi * GROUP_W, (gi + 1) * GROUP_W)))
        r1s = []
        for (bi, d, gi), (c, rows, lanes), st in zip(chains, sl, sts):
            w_r, qg_r = ins[d][1], ins[d][2]
            lhs1 = jnp.concatenate([w_r[0, bi, rows, lanes], qg_r[0, bi, rows, lanes]], axis=0)
            r1s.append(_dot(lhs1, _block_diag(st.astype(BF16), bdmask)))
        r2s = []
        for (bi, d, gi), (c, rows, lanes), r1 in zip(chains, sl, r1s):
            u_r, qk_r, kd_r = ins[d][0], ins[d][3], ins[d][4]
            v_new = u_r[0, bi, rows, lanes].astype(F32) - r1[0:CHUNK]
            lhs2 = jnp.concatenate([qk_r[0, bi, rows, lanes], kd_r[0, bi, rows, lanes]], axis=0)
            r2s.append(_dot(lhs2, _block_diag(v_new.astype(BF16), bdmask)))
        new_sts = []
        for (bi, d, gi), (c, rows, lanes), st, r1, r2 in zip(chains, sl, sts, r1s, r2s):
            eg_r, o_r = ins[d][5], ins[d][6]
            o_r[bi, rows, lanes] = (r1[CHUNK:2 * CHUNK] + r2[0:CHUNK]).astype(BF16)
            new_sts.append(st * eg_r[0, bi, 0, c:c + 1, lanes] + r2[CHUNK:2 * CHUNK])
        sts = new_sts
    for i, st in enumerate(sts):
        state[i] = st


def _gdn_scan(u, w, qg, qk, kdt, eg, bdmask):
    _, nb, s_len, _ = u.shape
    ns = s_len // TILE
    sb = SCAN_BATCH
    assert nb % sb == 0
    bwd = lambda j: jnp.where(j == 0, 0, ns - j)
    f_spec = pl.BlockSpec((1, sb, TILE, GDN_W), lambda b, j: (0, b, j, 0))
    b_spec = pl.BlockSpec((1, sb, TILE, GDN_W), lambda b, j: (1, b, bwd(j), 0))
    egf_spec = pl.BlockSpec((1, sb, 1, CHUNKS_PER_TILE, GDN_W), lambda b, j: (0, b, j, 0, 0))
    egb_spec = pl.BlockSpec((1, sb, 1, CHUNKS_PER_TILE, GDN_W), lambda b, j: (1, b, bwd(j), 0, 0))
    out_shape = jax.ShapeDtypeStruct((nb, s_len, GDN_W), BF16)
    return pl.pallas_call(
        _gdn_scan_kernel,
        grid=(nb // sb, ns),
        in_specs=[f_spec] * 5 + [egf_spec] + [b_spec] * 5 + [egb_spec] + [_const_spec(bdmask.shape)],
        out_specs=[pl.BlockSpec((sb, TILE, GDN_W), lambda b, j: (b, j, 0)),
                   pl.BlockSpec((sb, TILE, GDN_W), lambda b, j: (b, bwd(j), 0))],
        out_shape=[out_shape, out_shape],
        scratch_shapes=[pltpu.VMEM((sb * 2 * GDN_W // GROUP_W, CHUNK, GROUP_W), F32)],
        compiler_params=_cparams(2),
        name="gdn_scan",
    )(u, w, qg, qk, kdt, eg, u, w, qg, qk, kdt, eg, bdmask)


def _na_kernel(q_ref, k_ref, v_ref, bias_ref, o_ref, *, n_rows):
    s = pl.program_id(2)
    lane = lax.broadcasted_iota(jnp.int32, (1, 128), 1)
    head_mask = (lane < HEAD_DIM, lane >= HEAD_DIM)
    pair_lanes = [slice(128 * p, 128 * (p + 1)) for p in range(NA_PAIRS)]

    def stack_heads(q):
        return jnp.concatenate([jnp.where(head_mask[h], q, jnp.zeros_like(q)) for h in range(2)],
                               axis=0)

    def unstack_heads(o2, n):
        return jnp.where(head_mask[0], o2[0:n], o2[n:2 * n]).astype(BF16)

    @pl.when(s == 0)
    def _():
        q2s = [stack_heads(q_ref[0, :, ln]) for ln in pair_lanes]
        scs = [_dot_nt(q2, k_ref[0, 0:TILE, ln]) for q2, ln in zip(q2s, pair_lanes)]
        ps, dens = [], []
        for sc in scs:
            p = jnp.exp(sc - jnp.max(sc, axis=-1, keepdims=True))
            dens.append(jnp.sum(p, axis=-1, keepdims=True))
            ps.append(p.astype(BF16))
        os_ = [_dot(p, v_ref[0, 0:TILE, ln]) for p, ln in zip(ps, pair_lanes)]
        for o2, den, ln in zip(os_, dens, pair_lanes):
            o_ref[0, :, ln] = unstack_heads(o2 / den, TILE)

    @pl.when(s > 0)
    def _():
        rows_per_tile = TILE // GRID_W
        probs = []
        for p, ln in enumerate(pair_lanes):
            for r in range(rows_per_tile):
                row = (s - 1) * rows_per_tile + r
                rs = jnp.clip(row - NA_KH // 2, 0, n_rows - NA_KH)
                cls = rs - row + (NA_KH - 1)
                start = pl.multiple_of(TILE + rs * GRID_W, GRID_W)
                probs.append(dict(
                    p=p, r=r, ln=ln,
                    kw=k_ref[0, pl.ds(start, NA_KH * GRID_W), ln],
                    vw=v_ref[0, pl.ds(start, NA_KH * GRID_W), ln],
                    bias=jnp.concatenate([bias_ref[2 * p, cls], bias_ref[2 * p + 1, cls]], axis=0),
                    q2=stack_heads(q_ref[0, r * GRID_W:(r + 1) * GRID_W, ln])))
        sc_alls = [_dot_nt(jnp.concatenate([pb["q2"] for pb in probs if pb["p"] == p], axis=0),
                           k_ref[0, 0:TILE, ln]) for p, ln in enumerate(pair_lanes)]
        sws = [_dot_nt(pb["q2"], pb["kw"]) + pb["bias"] for pb in probs]
        pws, pcs, dens = [], [], []
        for pb, sw in zip(probs, sws):
            r = pb["r"]
            sc = sc_alls[pb["p"]][2 * GRID_W * r:2 * GRID_W * (r + 1)]
            mx = jnp.maximum(jnp.max(sw, axis=-1, keepdims=True), jnp.max(sc, axis=-1, keepdims=True))
            pw = jnp.exp(sw - mx)
            pc = jnp.exp(sc - mx)
            dens.append(jnp.sum(pw, axis=-1, keepdims=True) + jnp.sum(pc, axis=-1, keepdims=True))
            pws.append(pw.astype(BF16))
            pcs.append(pc.astype(BF16))
        oc_alls = [_dot(jnp.concatenate([pc for pb, pc in zip(probs, pcs) if pb["p"] == p], axis=0),
                        v_ref[0, 0:TILE, ln]) for p, ln in enumerate(pair_lanes)]
        ows = [_dot(pw, pb["vw"]) for pb, pw in zip(probs, pws)]
        for p, ln in enumerate(pair_lanes):
            outs = []
            for pb, ow, den in zip(probs, ows, dens):
                if pb["p"] == p:
                    r = pb["r"]
                    o2 = (ow + oc_alls[p][2 * GRID_W * r:2 * GRID_W * (r + 1)]) / den
                    outs.append(unstack_heads(o2, GRID_W))
            o_ref[0, :, ln] = jnp.concatenate(outs, axis=0)


def _na_attention(q, k, v, bias_tab):
    nb, s_len, _ = q.shape
    ns = s_len // TILE
    n_rows = (s_len - TILE) // GRID_W
    lanes = 128 * NA_PAIRS
    return pl.pallas_call(
        functools.partial(_na_kernel, n_rows=n_rows),
        grid=(nb, NA_W // lanes, ns),
        in_specs=[pl.BlockSpec((1, TILE, lanes), lambda b, hp, s: (b, s, hp)),
                  pl.BlockSpec((1, s_len, lanes), lambda b, hp, s: (b, 0, hp)),
                  pl.BlockSpec((1, s_len, lanes), lambda b, hp, s: (b, 0, hp)),
                  pl.BlockSpec((2 * NA_PAIRS, NA_KH, GRID_W, NA_KH * GRID_W),
                               lambda b, hp, s: (hp, 0, 0, 0))],
        out_specs=pl.BlockSpec((1, TILE, lanes), lambda b, hp, s: (b, s, hp)),
        out_shape=jax.ShapeDtypeStruct((nb, s_len, NA_W), BF16),
        compiler_params=_cparams(3),
        name="na_attn",
    )(q, k, v, bias_tab)


def _na_bias_tables(rpb):
    n_l, n_h, n_dr, n_dc = rpb.shape
    period = 2 * GRID_W
    left = (GRID_W - 1) - (NA_KW - 1)
    e = jnp.pad(rpb.astype(F32), ((0, 0), (0, 0), (0, 0), (left, period - n_dc - left)))
    skew = jnp.tile(e, (1, 1, 1, GRID_W))[..., :GRID_W * (period - 1)]
    skew = skew.reshape(n_l, n_h, n_dr, GRID_W, period - 1)[..., GRID_W - 1:]
    qi = np.arange(GRID_W)[:, None]
    kc = np.arange(GRID_W)[None, :]
    cs = np.clip(qi - NA_KW // 2, 0, GRID_W - NA_KW)
    in_win = np.logical_and(kc >= cs, kc < cs + NA_KW)
    toe = jnp.where(in_win, skew, MASK_NEG)
    toe = jnp.transpose(toe, (0, 1, 3, 2, 4))
    tab = jnp.stack([toe[:, :, :, c:c + NA_KH, :] for c in range(NA_KH)], axis=2)
    return tab.reshape(n_l, n_h, NA_KH, GRID_W, NA_KH * GRID_W)


def _post_mixer_kernel(x_ref, mod_ref, ona_ref, of_ref, ob_ref, z_ref, gg_ref, gp_ref, w_ref,
                       ones_ref, o_ref):
    d = x_ref.shape[-1]
    gate = mod_ref[0][:, 2 * d:3 * d]
    ones_bd = ones_ref[...]
    blocks = [slice(i * POST_ROWS, (i + 1) * POST_ROWS) for i in range(TILE // POST_ROWS)]
    ogs, mss = [], []
    for rows in blocks:
        og = of_ref[0, rows, :].astype(F32) + ob_ref[0, rows, :].astype(F32)
        sq = og * og
        hi = sq.astype(BF16)
        lo = (sq - hi.astype(F32)).astype(BF16)
        ogs.append(og)
        mss.append((_dot(hi, ones_bd) + _dot(lo, ones_bd)) * (1.0 / HEAD_DIM))
    ys = []
    for rows, og, ms in zip(blocks, ogs, mss):
        gated = og * lax.rsqrt(ms + RMS_EPS) * gg_ref[...] * _silu(z_ref[0, rows, :].astype(F32))
        ys.append(_dot(ona_ref[0, rows, :], w_ref[0:NA_W, :])
                  + _dot(gated.astype(BF16), w_ref[NA_W:NA_W + GDN_W, :]))
    for rows, y in zip(blocks, ys):
        o_ref[0, rows, :] = x_ref[0, rows, :] + gate * _rms(y, gp_ref[...])


def _post_mixer(x_all, mod_l, o_na, o_f, o_b, z, gdn_gain, post_gain, w_out, ones_bd):
    nb, s_len, d = x_all.shape
    ns = s_len // TILE
    tok = lambda w: pl.BlockSpec((1, TILE, w), lambda b, s: (b, s, 0))
    return pl.pallas_call(
        _post_mixer_kernel,
        grid=(nb, ns),
        in_specs=[tok(d),
                  pl.BlockSpec((1, 1, N_MOD * d), lambda b, s: (_mod_row(b, s, nb), 0, 0)),
                  tok(NA_W), tok(GDN_W), tok(GDN_W), tok(GDN_W),
                  _const_spec(gdn_gain.shape), _const_spec(post_gain.shape),
                  _const_spec(w_out.shape), _const_spec(ones_bd.shape)],
        out_specs=tok(d),
        out_shape=jax.ShapeDtypeStruct(x_all.shape, F32),
        input_output_aliases={0: 0},
        compiler_params=_cparams(2),
        name="post_mixer",
    )(x_all, mod_l, o_na, o_f, o_b, z, gdn_gain, post_gain, w_out, ones_bd)


def _conv_ffn_kernel(x_ref, xp_ref, xn_ref, mod_ref, gpre_ref, gpost_ref, wu_ref, cw_ref, wd_ref,
                     o_ref, act_s, *, first_block):
    s = pl.program_id(1) + first_block
    ns = pl.num_programs(1) + first_block
    d = x_ref.shape[-1]
    d_ff = wd_ref.shape[0]
    m = mod_ref[0]
    shift, scale, gate = m[:, 3 * d:4 * d], m[:, 4 * d:5 * d], m[:, 5 * d:6 * d]
    x = x_ref[0]
    prev_ok = (s >= 2).astype(F32)
    next_ok = jnp.logical_and(s >= 1, s < ns - 1).astype(F32)

    def hidden(t):
        return _rms(t, gpre_ref[...]) * (1.0 + scale) + shift

    h = jnp.concatenate([hidden(xp_ref[0]) * prev_ok, hidden(x), hidden(xn_ref[0]) * next_ok],
                        axis=0).astype(BF16)
    n_steps = d_ff // FFN_COLS

    def up(c):
        return [_dot(h, wu_ref[:, off:off + FFN_COLS]) for off in (c * FFN_COLS, d_ff + c * FFN_COLS)]

    def activation(c, us):
        halves = []
        n_rows = TILE + 2 * HALO
        for off, u in zip((c * FFN_COLS, d_ff + c * FFN_COLS), us):
            u_m1 = pltpu.roll(u, 1, 0)[HALO:HALO + TILE]
            u_p1 = pltpu.roll(u, n_rows - 1, 0)[HALO:HALO + TILE]
            halves.append(cw_ref[0:1, off:off + FFN_COLS] * u_m1
                          + cw_ref[1:2, off:off + FFN_COLS] * u[HALO:HALO + TILE]
                          + cw_ref[2:3, off:off + FFN_COLS] * u_p1)
        return (_silu(halves[0]) * halves[1]).astype(BF16)

    u_next = up(0)
    for c in range(n_steps):
        u_cur = u_next
        if c + 1 < n_steps:
            u_next = up(c + 1)
        act_s[:, c * FFN_COLS:(c + 1) * FFN_COLS] = activation(c, u_cur)
    acc = _dot(act_s[...], wd_ref[...])
    o_ref[0] = x + gate * _rms(acc, gpost_ref[...])


def _conv_ffn(x_all, mod_l, pre_gain, post_gain, w_up, conv_w, w_down, latent_only):
    nb, s_len, d = x_all.shape
    ns = s_len // TILE
    hpt = TILE // HALO
    nh = s_len // HALO
    first = 1 if latent_only else 0
    return pl.pallas_call(
        functools.partial(_conv_ffn_kernel, first_block=first),
        grid=(nb, ns - first),
        in_specs=[pl.BlockSpec((1, TILE, d), lambda b, s: (b, s + first, 0)),
                  pl.BlockSpec((1, HALO, d),
                               lambda b, s: (b, jnp.maximum((s + first) * hpt - 1, 0), 0)),
                  pl.BlockSpec((1, HALO, d),
                               lambda b, s: (b, jnp.minimum((s + first + 1) * hpt, nh - 1), 0)),
                  pl.BlockSpec((1, 1, N_MOD * d),
                               lambda b, s: (_mod_row(b, s + first, nb), 0, 0)),
                  _const_spec(pre_gain.shape), _const_spec(post_gain.shape),
                  _const_spec(w_up.shape), _const_spec(conv_w.shape), _const_spec(w_down.shape)],
        out_specs=pl.BlockSpec((1, TILE, d), lambda b, s: (b, s, 0)),
        out_shape=jax.ShapeDtypeStruct((nb, s_len - first * TILE, d), F32),
        scratch_shapes=[pltpu.VMEM((TILE, w_down.shape[0]), BF16)],
        compiler_params=_cparams(2),
        name="conv_ffn",
    )(x_all, x_all, x_all, mod_l, pre_gain, post_gain, w_up, conv_w, w_down)


def _rotary_tables(n_ctx, n_lat):
    t = jnp.arange(n_lat)
    row = (t // GRID_W).astype(F32)
    col = (t % GRID_W).astype(F32)
    pairs = HEAD_DIM // 4
    inv_freq = ROPE_BASE ** (-jnp.arange(pairs, dtype=F32) / pairs)
    ang = jnp.concatenate([row[:, None] * inv_freq, col[:, None] * inv_freq], axis=-1)
    cos, sin = jnp.cos(ang), jnp.sin(ang)
    cos64 = jnp.concatenate([cos, cos], axis=-1)
    sin64 = jnp.concatenate([-sin, sin], axis=-1)
    cos_t = jnp.concatenate([jnp.ones((n_ctx, HEAD_DIM), F32), cos64], axis=0)
    sin_t = jnp.concatenate([jnp.zeros((n_ctx, HEAD_DIM), F32), sin64], axis=0)
    return jnp.tile(cos_t, (1, 2)), jnp.tile(sin_t, (1, 2))


def _static_masks():
    lane_h = np.arange(GROUP_W) // HEAD_DIM
    bdmask = (lane_h[:, None] == lane_h[None, :]).astype(np.float32)
    seg = np.arange(GDN_W) // HEAD_DIM
    ones_bd = (seg[:, None] == seg[None, :]).astype(np.float32)
    i = np.arange(CHUNK)[:, None]
    j = (np.arange(GROUP_W) % HEAD_DIM)[None, :]
    lvl = []
    for lv in range(N_LEVELS):
        n = 2 ** lv
        lvl.append(np.logical_and(i // (2 * n) == j // (2 * n), i // n != j // n))
    lvl.append(i == j)
    lvl = np.stack(lvl).astype(np.float32)
    lvlbd = np.tile(lvl[:N_LEVELS], (1, HEADS_PER_GROUP, 1)) * bdmask[None]
    return (jnp.asarray(bdmask, BF16), jnp.asarray(ones_bd, BF16), jnp.asarray(lvl, F32),
            jnp.asarray(lvlbd, BF16))


def _gate_lane_vector(p):
    flat = p.reshape(1, 2 * GDN_HEADS).astype(F32)
    return jnp.pad(flat, ((0, 0), (2 * GDN_HEADS, GATE_PAD - 4 * GDN_HEADS)))


def kernel(x, c, ctx, c_ctx, ada_w, ada_b, norm_mix_pre, norm_mix_post, w_in, qkv_conv, a_log,
           dt_bias, gdn_norm, rpb, w_out, norm_ffn_pre, norm_ffn_post, ffn_up, ffn_conv, ffn_down):
    nb, n_lat, d = x.shape
    n_ctx = ctx.shape[1]
    depth = ada_w.shape[0]
    assert n_ctx == TILE and n_lat % TILE == 0 and n_lat // GRID_W >= NA_KH

    mod_rows = -(-(nb + 1) // 8) * 8
    cc = jnp.concatenate([c, c_ctx[None, :], jnp.zeros((mod_rows - nb - 1, d), F32)], axis=0)
    mod_all = _modulation(cc, ada_w, ada_b)

    cos_t, sin_t = _rotary_tables(n_ctx, n_lat)
    bdmask, ones_bd, lvl, lvlbd = _static_masks()
    in_w = w_in.shape[-1]
    w_in_p = jnp.pad(w_in, ((0, 0), (0, 0), (0, GATE_PAD - (in_w - 3 * NA_W - 4 * GDN_W)))).astype(BF16)
    w_out_b = w_out.astype(BF16)
    w_up_b = ffn_up.astype(BF16)
    w_down_b = ffn_down.astype(BF16)
    bias_tabs = _na_bias_tables(rpb)

    x_all = jnp.concatenate([ctx, x], axis=1)
    for l in range(depth):
        mod_l = mod_all[l].reshape(mod_rows, 1, N_MOD * d)
        naq, nak, nav, gqkv, z, gates = _pre_mixer(x_all, mod_l, norm_mix_pre[l][None, :], w_in_p[l])
        u, w, qg, qk, kdt, eg = _gdn_local(gqkv, gates, qkv_conv[l], _gate_lane_vector(a_log[l]),
                                           _gate_lane_vector(dt_bias[l]), cos_t, sin_t, ones_bd,
                                           bdmask, lvl, lvlbd)
        o_f, o_b = _gdn_scan(u, w, qg, qk, kdt, eg, bdmask)
        o_na = _na_attention(naq, nak, nav, bias_tabs[l])
        x_all = _post_mixer(x_all, mod_l, o_na, o_f, o_b, z, jnp.tile(gdn_norm[l], GDN_HEADS)[None, :],
                            norm_mix_post[l][None, :], w_out_b[l], ones_bd)
        x_all = _conv_ffn(x_all, mod_l, norm_ffn_pre[l][None, :], norm_ffn_post[l][None, :],
                          w_up_b[l], ffn_conv[l], w_down_b[l], latent_only=(l == depth - 1))
    return x_all
```

```python
import functools

import jax
import jax.numpy as jnp
import numpy as np
from jax import lax
from jax.experimental import pallas as pl
from jax.experimental.pallas import tpu as pltpu

F32 = jnp.float32
BF16 = jnp.bfloat16

HEAD_DIM = 64
NA_HEADS = 8
GDN_HEADS = 8
NA_W = NA_HEADS * HEAD_DIM
GDN_W = GDN_HEADS * HEAD_DIM
GRID_W = 64
NA_KH = 8
NA_KW = 16
CHUNK = 64
ROPE_BASE = 10000.0
RMS_EPS = 1e-6
N_MOD = 6
TILE = 256
CHUNKS_PER_TILE = TILE // CHUNK
HALO = 8
GROUP_W = 128
HEADS_PER_GROUP = GROUP_W // HEAD_DIM
N_LEVELS = 6
LOCAL_PAR_CHUNKS = 2
NA_PAIRS = 2
POST_ROWS = 128
SCAN_BATCH = 2
MASK_NEG = -1e30
GATE_PAD = 128
FFN_COLS = 256
VMEM_LIMIT = 56 * 1024 * 1024


def _cparams(n_axes):
    return pltpu.CompilerParams(dimension_semantics=("arbitrary",) * n_axes,
                                vmem_limit_bytes=VMEM_LIMIT)


def _const_spec(shape):
    nd = len(shape)
    return pl.BlockSpec(shape, lambda *_: (0,) * nd)


def _layer_spec(arr, layer):
    nd = arr.ndim
    return pl.BlockSpec((None,) + arr.shape[1:], lambda *_: (layer,) + (0,) * (nd - 1))


def _mod_spec(mod_all, layer, n_batch, first=0):
    return pl.BlockSpec((None, 1, 1, mod_all.shape[-1]),
                        lambda b, s: (layer, _mod_row(b, s + first, n_batch), 0, 0))


def _stream_specs(stream, d):
    if isinstance(stream, tuple):
        ctx, lat = stream
        return ([pl.BlockSpec((1, TILE, d), lambda b, s: (b, 0, 0)),
                 pl.BlockSpec((1, TILE, d), lambda b, s: (b, jnp.maximum(s - 1, 0), 0))], [ctx, lat])
    return [pl.BlockSpec((1, TILE, d), lambda b, s: (b, s, 0))], [stream]


def _stream_tile(refs):
    if len(refs) == 2:
        return jnp.where(pl.program_id(1) == 0, refs[0][0], refs[1][0])
    return refs[0][0]


def _sigmoid(x):
    return 1.0 / (1.0 + jnp.exp(-x))


def _silu(x):
    return x * _sigmoid(x)


def _softplus(x):
    return jnp.maximum(x, 0.0) + jnp.log(1.0 + jnp.exp(-jnp.abs(x)))


def _rms(x, gain):
    ms = jnp.mean(x * x, axis=-1, keepdims=True)
    return x * lax.rsqrt(ms + RMS_EPS) * gain


def _dot(a, b):
    return jnp.dot(a, b, preferred_element_type=F32)


def _dot_nt(a, b):
    return lax.dot_general(a, b, (((1,), (1,)), ((), ())), preferred_element_type=F32)


def _mod_row(b, s, n_batch):
    return jnp.where(s == 0, n_batch, b)


def _modulation_kernel(c_ref, w_ref, b_ref, o_ref):
    a = _silu(c_ref[...]).astype(BF16)
    o_ref[0] = _dot(a, w_ref[0].astype(BF16)) + b_ref[0]


def _modulation(cc, ada_w, ada_b):
    depth, d, n = ada_w.shape
    rows = cc.shape[0]
    nblk = n // N_MOD
    return pl.pallas_call(
        _modulation_kernel,
        grid=(depth, N_MOD),
        in_specs=[pl.BlockSpec((rows, d), lambda l, j: (0, 0)),
                  pl.BlockSpec((1, d, nblk), lambda l, j: (l, 0, j)),
                  pl.BlockSpec((1, 1, nblk), lambda l, j: (l, 0, j))],
        out_specs=pl.BlockSpec((1, rows, nblk), lambda l, j: (l, 0, j)),
        out_shape=jax.ShapeDtypeStruct((depth, rows, n), F32),
        compiler_params=_cparams(2),
        name="modulation",
    )(cc, ada_w, ada_b.reshape(depth, 1, n))


def _pre_mixer_kernel(*refs):
    mod_ref, g_ref, w_ref, q_ref, k_ref, v_ref, gq_ref, z_ref, gt_ref = refs[-9:]
    x = _stream_tile(refs[:-9])
    d = x.shape[-1]
    m = mod_ref[0]
    h = _rms(x, g_ref[...]) * (1.0 + m[:, d:2 * d]) + m[:, 0:d]
    hb = h.astype(BF16)
    o = 0
    q_ref[0] = (_dot(hb, w_ref[:, o:o + NA_W]) * HEAD_DIM ** -0.5).astype(BF16)
    o += NA_W
    k_ref[0] = _dot(hb, w_ref[:, o:o + NA_W]).astype(BF16)
    o += NA_W
    v_ref[0] = _dot(hb, w_ref[:, o:o + NA_W]).astype(BF16)
    o += NA_W
    gq_ref[0] = _dot(hb, w_ref[:, o:o + 3 * GDN_W]).astype(BF16)
    o += 3 * GDN_W
    z_ref[0] = _dot(hb, w_ref[:, o:o + GDN_W]).astype(BF16)
    o += GDN_W
    gt_ref[0] = _dot(hb, w_ref[:, o:o + GATE_PAD])


def _pre_mixer(stream, s_len, mod_all, gain, w_in_p, layer):
    x_specs, x_args = _stream_specs(stream, gain.shape[-1])
    nb, d = x_args[0].shape[0], gain.shape[-1]
    ns = s_len // TILE
    tok = lambda w: pl.BlockSpec((1, TILE, w), lambda b, s: (b, s, 0))
    shp = lambda w, dt: jax.ShapeDtypeStruct((nb, s_len, w), dt)
    return pl.pallas_call(
        _pre_mixer_kernel,
        grid=(nb, ns),
        in_specs=x_specs + [_mod_spec(mod_all, layer, nb),
                            _const_spec((1, d)),
                            _layer_spec(w_in_p, layer)],
        out_specs=[tok(NA_W), tok(NA_W), tok(NA_W), tok(3 * GDN_W), tok(GDN_W), tok(GATE_PAD)],
        out_shape=[shp(NA_W, BF16), shp(NA_W, BF16), shp(NA_W, BF16), shp(3 * GDN_W, BF16),
                   shp(GDN_W, BF16), shp(GATE_PAD, F32)],
        compiler_params=_cparams(2),
        name="pre_mixer",
    )(*x_args, mod_all, gain, w_in_p)


def _block_diag(x, bdmask):
    return jnp.concatenate([x] * HEADS_PER_GROUP, axis=0) * bdmask


def _split3(x):
    a = x.astype(BF16)
    r = x - a.astype(F32)
    b = r.astype(BF16)
    c = (r - b.astype(F32)).astype(BF16)
    return a, b, c


def _expand_heads(a, c0):
    rows = a.shape[0]
    lane = lax.broadcasted_iota(jnp.int32, (rows, 128), 1)
    parts = []
    for p in range(GDN_HEADS // 2):
        lo = jnp.broadcast_to(a[:, c0 + 2 * p:c0 + 2 * p + 1], (rows, 128))
        hi = jnp.broadcast_to(a[:, c0 + 2 * p + 1:c0 + 2 * p + 2], (rows, 128))
        parts.append(jnp.where(lane < HEAD_DIM, lo, hi))
    return jnp.concatenate(parts, axis=1)


def _gdn_local_kernel(x_ref, xp_ref, xn_ref, gt_ref, cw_ref, av_ref, dt_ref, cos_ref, sin_ref,
                      ones_ref, bd_ref, lvl_ref, lvlbd_ref,
                      u_ref, w_ref, qg_ref, qk_ref, kdt_ref, eg_ref,
                      q_s, k_s, v_s, gc_s, be_s):
    s = pl.program_id(1)
    ns = pl.num_programs(1)
    width = x_ref.shape[-1]

    x = x_ref[0].astype(F32)
    prev_ok = (s >= 2).astype(F32)
    next_ok = jnp.logical_and(s >= 1, s < ns - 1).astype(F32)
    prev_row = xp_ref[0, HALO - 1:HALO, :].astype(F32) * prev_ok
    next_row = xn_ref[0, 0:1, :].astype(F32) * next_ok
    ri = lax.broadcasted_iota(jnp.int32, (TILE, width), 0)
    x_m1 = jnp.where(ri == 0, prev_row, pltpu.roll(x, 1, 0))
    x_p1 = jnp.where(ri == TILE - 1, next_row, pltpu.roll(x, TILE - 1, 0))
    y = cw_ref[0:1, :] * x_m1 + cw_ref[1:2, :] * x + cw_ref[2:3, :] * x_p1
    a = _silu(y)
    q = a[:, 0:GDN_W]
    k = a[:, GDN_W:2 * GDN_W]
    v_s[...] = a[:, 2 * GDN_W:3 * GDN_W]

    ones_bd = ones_ref[...]

    def seg_sum(t):
        return _dot(t.astype(BF16), ones_bd)

    q = q * lax.rsqrt(seg_sum(q * q) + RMS_EPS)
    k = k * lax.rsqrt(seg_sum(k * k) + RMS_EPS)
    cos_t = cos_ref[...]
    sin_t = sin_ref[...]
    lane = lax.broadcasted_iota(jnp.int32, (TILE, 128), 1)
    first_half = (lane % HEAD_DIM) < HEAD_DIM // 2

    def rope(t):
        parts = []
        for p in range(GDN_W // 128):
            ts = t[:, 128 * p:128 * (p + 1)]
            partner = jnp.where(first_half, pltpu.roll(ts, 128 - HEAD_DIM // 2, 1),
                                pltpu.roll(ts, HEAD_DIM // 2, 1))
            parts.append(ts * cos_t + partner * sin_t)
        return jnp.concatenate(parts, axis=1)

    q_s[...] = rope(q) * HEAD_DIM ** -0.5
    k_s[...] = rope(k)

    gt = gt_ref[0]
    beta = _sigmoid(gt)
    g_raw = -jnp.exp(av_ref[...]) * _softplus(gt + dt_ref[...])
    r_i = lax.broadcasted_iota(jnp.int32, (TILE, TILE), 0)
    c_i = lax.broadcasted_iota(jnp.int32, (TILE, TILE), 1)
    same_chunk = (r_i // CHUNK) == (c_i // CHUNK)
    tri_f = jnp.where(jnp.logical_and(same_chunk, r_i >= c_i), 1.0, 0.0).astype(BF16)
    tri_b = jnp.where(jnp.logical_and(same_chunk, r_i <= c_i), 1.0, 0.0).astype(BF16)
    g1, g2, g3 = _split3(g_raw)
    cum_f = _dot(tri_f, g1) + _dot(tri_f, g2) + _dot(tri_f, g3)
    cum_b = _dot(tri_b, g1) + _dot(tri_b, g2) + _dot(tri_b, g3)
    gc_s[0] = _expand_heads(cum_f, 2 * GDN_HEADS)
    gc_s[1] = _expand_heads(cum_b, 3 * GDN_HEADS)
    be_s[0] = _expand_heads(beta, 0)
    be_s[1] = _expand_heads(beta, GDN_HEADS)

    bdmask = bd_ref[...]
    eye = lvl_ref[N_LEVELS]
    ii = lax.broadcasted_iota(jnp.int32, (CHUNK, GROUP_W), 0)
    jj = lax.broadcasted_iota(jnp.int32, (CHUNK, GROUP_W), 1) % HEAD_DIM

    def setup(c, d, gi):
        rows = slice(c * CHUNK, (c + 1) * CHUNK)
        lanes = slice(gi * GROUP_W, (gi + 1) * GROUP_W)
        kk = k_s[rows, lanes]
        qq = q_s[rows, lanes]
        gc = gc_s[d, rows, lanes]
        be = be_s[d, rows, lanes]
        tri = (ii >= jj) if d == 0 else (ii <= jj)
        g_last = gc[CHUNK - 1:CHUNK, :] if d == 0 else gc[0:1, :]
        kb = kk * be
        e_gc = jnp.exp(gc)
        gc_t = jnp.sum(gc * eye, axis=0, keepdims=True)
        dec = jnp.exp(jnp.where(tri, gc - gc_t, MASK_NEG))
        qg_ref[d, 0, rows, lanes] = (qq * e_gc).astype(BF16)
        return dict(c=c, rows=rows, lanes=lanes, d=d, kk=kk, dec=dec, g_last=g_last,
                    lhs=jnp.concatenate([kb, qq], axis=0).astype(BF16),
                    vb=(v_s[rows, lanes] * be).astype(BF16), kbg=(kb * e_gc).astype(BF16),
                    kd=(kk * jnp.exp(g_last - gc)).astype(BF16))

    for c0 in range(0, CHUNKS_PER_TILE, LOCAL_PAR_CHUNKS):
        ps = [setup(c, d, gi) for c in range(c0, c0 + LOCAL_PAR_CHUNKS) for d in range(2)
              for gi in range(GDN_W // GROUP_W)]
        grams = [_dot_nt(p["lhs"], _block_diag(p["kk"].astype(BF16), bdmask)) for p in ps]
        ms = []
        for p, gram in zip(ps, grams):
            ms.append(gram[0:CHUNK] * p["dec"])
            qk_ref[p["d"], 0, p["rows"], p["lanes"]] = (gram[CHUNK:2 * CHUNK] * p["dec"]).astype(BF16)
        xs = [eye - m * lvl_ref[0] for m in ms]
        xb = [x.astype(BF16) for x in xs]
        mt = [jnp.concatenate([m.astype(BF16)] * HEADS_PER_GROUP, axis=0) for m in ms]
        for lv in range(1, N_LEVELS):
            ys = [_dot(x, m * lvlbd_ref[lv]) for x, m in zip(xb, mt)]
            zs = [_dot(y.astype(BF16), _block_diag(x, bdmask)) for y, x in zip(ys, xb)]
            xs = [x - z for x, z in zip(xs, zs)]
            xb = [x.astype(BF16) for x in xs]
        uws = [_dot(x, jnp.concatenate([_block_diag(p["vb"], bdmask), _block_diag(p["kbg"], bdmask)],
                                       axis=1)) for x, p in zip(xb, ps)]
        us = [uw[:, 0:GROUP_W] for uw in uws]
        ws = [uw[:, GROUP_W:2 * GROUP_W] for uw in uws]
        kdts = [_dot_nt(eye.astype(BF16), _block_diag(p["kd"], bdmask)) for p in ps]
        for p, u, w, kdt in zip(ps, us, ws, kdts):
            u_ref[p["d"], 0, p["rows"], p["lanes"]] = u.astype(BF16)
            w_ref[p["d"], 0, p["rows"], p["lanes"]] = w.astype(BF16)
            kdt_ref[p["d"], 0, p["rows"], p["lanes"]] = kdt.astype(BF16)
            eg_ref[p["d"], 0, 0, p["c"]:p["c"] + 1, p["lanes"]] = jnp.exp(p["g_last"])


def _gdn_local(gqkv, gates, conv_w, avec, dtvec, cos_t, sin_t, ones_bd, bdmask, lvl, lvlbd):
    nb, s_len, width = gqkv.shape
    ns = s_len // TILE
    hpt = TILE // HALO
    nh = s_len // HALO
    dir_spec = pl.BlockSpec((2, 1, TILE, GDN_W), lambda b, s: (0, b, s, 0))
    dir_shape = jax.ShapeDtypeStruct((2, nb, s_len, GDN_W), BF16)
    return pl.pallas_call(
        _gdn_local_kernel,
        grid=(nb, ns),
        in_specs=[pl.BlockSpec((1, TILE, width), lambda b, s: (b, s, 0)),
                  pl.BlockSpec((1, HALO, width), lambda b, s: (b, jnp.maximum(s * hpt - 1, 0), 0)),
                  pl.BlockSpec((1, HALO, width),
                               lambda b, s: (b, jnp.minimum((s + 1) * hpt, nh - 1), 0)),
                  pl.BlockSpec((1, TILE, GATE_PAD), lambda b, s: (b, s, 0)),
                  _const_spec(conv_w.shape),
                  _const_spec(avec.shape),
                  _const_spec(dtvec.shape),
                  pl.BlockSpec((TILE, 128), lambda b, s: (s, 0)),
                  pl.BlockSpec((TILE, 128), lambda b, s: (s, 0)),
                  _const_spec(ones_bd.shape),
                  _const_spec(bdmask.shape),
                  _const_spec(lvl.shape),
                  _const_spec(lvlbd.shape)],
        out_specs=[dir_spec] * 5 + [pl.BlockSpec((2, 1, 1, CHUNKS_PER_TILE, GDN_W),
                                                 lambda b, s: (0, b, s, 0, 0))],
        out_shape=[dir_shape] * 5 + [jax.ShapeDtypeStruct((2, nb, ns, CHUNKS_PER_TILE, GDN_W), F32)],
        scratch_shapes=[pltpu.VMEM((TILE, GDN_W), F32), pltpu.VMEM((TILE, GDN_W), F32),
                        pltpu.VMEM((TILE, GDN_W), F32), pltpu.VMEM((2, TILE, GDN_W), F32),
                        pltpu.VMEM((2, TILE, GDN_W), F32)],
        compiler_params=_cparams(2),
        name="gdn_local",
    )(gqkv, gqkv, gqkv, gates, conv_w, avec, dtvec, cos_t, sin_t, ones_bd, bdmask, lvl, lvlbd)


def _gdn_scan_kernel(uf, wf, qgf, qkf, kdf, egf, ub, wb, qgb, qkb, kdb, egb, bd_ref,
                     of_ref, ob_ref, state):
    j = pl.program_id(1)

    @pl.when(j == 0)
    def _():
        state[...] = jnp.zeros_like(state)

    bdmask = bd_ref[...]
    ins = ((uf, wf, qgf, qkf, kdf, egf, of_ref), (ub, wb, qgb, qkb, kdb, egb, ob_ref))
    n_groups = GDN_W // GROUP_W
    chains = [(bi, d, gi) for bi in range(SCAN_BATCH) for d in range(2) for gi in range(n_groups)]
    sts = [state[i] for i in range(len(chains))]
    for ci in range(CHUNKS_PER_TILE):
        sl = []
        for bi, d, gi in chains:
            c = ci if d == 0 else CHUNKS_PER_TILE - 1 - ci
            sl.append((c, slice(c * CHUNK, (c + 1) * CHUNK), slice(gi * GROUP_W, (gi + 1) * GROUP_W)))
        r1s = []
        for (bi, d, gi), (c, rows, lanes), st in zip(chains, sl, sts):
            w_r, qg_r = ins[d][1], ins[d][2]
            lhs1 = jnp.concatenate([w_r[0, bi, rows, lanes], qg_r[0, bi, rows, lanes]], axis=0)
            r1s.append(_dot(lhs1, _block_diag(st.astype(BF16), bdmask)))
        r2s = []
        for (bi, d, gi), (c, rows, lanes), r1 in zip(chains, sl, r1s):
            u_r, qk_r, kd_r = ins[d][0], ins[d][3], ins[d][4]
            v_new = u_r[0, bi, rows, lanes].astype(F32) - r1[0:CHUNK]
            lhs2 = jnp.concatenate([qk_r[0, bi, rows, lanes], kd_r[0, bi, rows, lanes]], axis=0)
            r2s.append(_dot(lhs2, _block_diag(v_new.astype(BF16), bdmask)))
        new_sts = []
        for (bi, d, gi), (c, rows, lanes), st, r1, r2 in zip(chains, sl, sts, r1s, r2s):
            eg_r, o_r = ins[d][5], ins[d][6]
            o_r[bi, rows, lanes] = (r1[CHUNK:2 * CHUNK] + r2[0:CHUNK]).astype(BF16)
            new_sts.append(st * eg_r[0, bi, 0, c:c + 1, lanes] + r2[CHUNK:2 * CHUNK])
        sts = new_sts
    for i, st in enumerate(sts):
        state[i] = st


def _gdn_scan(u, w, qg, qk, kdt, eg, bdmask):
    _, nb, s_len, _ = u.shape
    ns = s_len // TILE
    sb = SCAN_BATCH
    assert nb % sb == 0
    bwd = lambda j: jnp.where(j == 0, 0, ns - j)
    f_spec = pl.BlockSpec((1, sb, TILE, GDN_W), lambda b, j: (0, b, j, 0))
    b_spec = pl.BlockSpec((1, sb, TILE, GDN_W), lambda b, j: (1, b, bwd(j), 0))
    egf_spec = pl.BlockSpec((1, sb, 1, CHUNKS_PER_TILE, GDN_W), lambda b, j: (0, b, j, 0, 0))
    egb_spec = pl.BlockSpec((1, sb, 1, CHUNKS_PER_TILE, GDN_W), lambda b, j: (1, b, bwd(j), 0, 0))
    out_shape = jax.ShapeDtypeStruct((nb, s_len, GDN_W), BF16)
    return pl.pallas_call(
        _gdn_scan_kernel,
        grid=(nb // sb, ns),
        in_specs=[f_spec] * 5 + [egf_spec] + [b_spec] * 5 + [egb_spec] + [_const_spec(bdmask.shape)],
        out_specs=[pl.BlockSpec((sb, TILE, GDN_W), lambda b, j: (b, j, 0)),
                   pl.BlockSpec((sb, TILE, GDN_W), lambda b, j: (b, bwd(j), 0))],
        out_shape=[out_shape, out_shape],
        scratch_shapes=[pltpu.VMEM((sb * 2 * GDN_W // GROUP_W, CHUNK, GROUP_W), F32)],
        compiler_params=_cparams(2),
        name="gdn_scan",
    )(u, w, qg, qk, kdt, eg, u, w, qg, qk, kdt, eg, bdmask)


def _na_kernel(q_ref, k_ref, v_ref, bias_ref, o_ref, *, n_rows):
    s = pl.program_id(2)
    lane = lax.broadcasted_iota(jnp.int32, (1, 128), 1)
    head_mask = (lane < HEAD_DIM, lane >= HEAD_DIM)
    pair_lanes = [slice(128 * p, 128 * (p + 1)) for p in range(NA_PAIRS)]

    def stack_heads(q):
        return jnp.concatenate([jnp.where(head_mask[h], q, jnp.zeros_like(q)) for h in range(2)],
                               axis=0)

    def unstack_heads(o2, n):
        return jnp.where(head_mask[0], o2[0:n], o2[n:2 * n]).astype(BF16)

    @pl.when(s == 0)
    def _():
        q2s = [stack_heads(q_ref[0, :, ln]) for ln in pair_lanes]
        scs = [_dot_nt(q2, k_ref[0, 0:TILE, ln]) for q2, ln in zip(q2s, pair_lanes)]
        ps, dens = [], []
        for sc in scs:
            p = jnp.exp(sc - jnp.max(sc, axis=-1, keepdims=True))
            dens.append(jnp.sum(p, axis=-1, keepdims=True))
            ps.append(p.astype(BF16))
        os_ = [_dot(p, v_ref[0, 0:TILE, ln]) for p, ln in zip(ps, pair_lanes)]
        for o2, den, ln in zip(os_, dens, pair_lanes):
            o_ref[0, :, ln] = unstack_heads(o2 / den, TILE)

    @pl.when(s > 0)
    def _():
        rows_per_tile = TILE // GRID_W
        probs = []
        for p, ln in enumerate(pair_lanes):
            for r in range(rows_per_tile):
                row = (s - 1) * rows_per_tile + r
                rs = jnp.clip(row - NA_KH // 2, 0, n_rows - NA_KH)
                cls = rs - row + (NA_KH - 1)
                start = pl.multiple_of(TILE + rs * GRID_W, GRID_W)
                probs.append(dict(
                    p=p, r=r, ln=ln,
                    kw=k_ref[0, pl.ds(start, NA_KH * GRID_W), ln],
                    vw=v_ref[0, pl.ds(start, NA_KH * GRID_W), ln],
                    bias=jnp.concatenate([bias_ref[2 * p, cls], bias_ref[2 * p + 1, cls]], axis=0),
                    q2=stack_heads(q_ref[0, r * GRID_W:(r + 1) * GRID_W, ln])))
        sc_alls = [_dot_nt(jnp.concatenate([pb["q2"] for pb in probs if pb["p"] == p], axis=0),
                           k_ref[0, 0:TILE, ln]) for p, ln in enumerate(pair_lanes)]
        sws = [_dot_nt(pb["q2"], pb["kw"]) + pb["bias"] for pb in probs]
        pws, pcs, dens = [], [], []
        for pb, sw in zip(probs, sws):
            r = pb["r"]
            sc = sc_alls[pb["p"]][2 * GRID_W * r:2 * GRID_W * (r + 1)]
            mx = jnp.maximum(jnp.max(sw, axis=-1, keepdims=True), jnp.max(sc, axis=-1, keepdims=True))
            pw = jnp.exp(sw - mx)
            pc = jnp.exp(sc - mx)
            dens.append(jnp.sum(pw, axis=-1, keepdims=True) + jnp.sum(pc, axis=-1, keepdims=True))
            pws.append(pw.astype(BF16))
            pcs.append(pc.astype(BF16))
        oc_alls = [_dot(jnp.concatenate([pc for pb, pc in zip(probs, pcs) if pb["p"] == p], axis=0),
                        v_ref[0, 0:TILE, ln]) for p, ln in enumerate(pair_lanes)]
        ows = [_dot(pw, pb["vw"]) for pb, pw in zip(probs, pws)]
        for p, ln in enumerate(pair_lanes):
            outs = []
            for pb, ow, den in zip(probs, ows, dens):
                if pb["p"] == p:
                    r = pb["r"]
                    o2 = (ow + oc_alls[p][2 * GRID_W * r:2 * GRID_W * (r + 1)]) / den
                    outs.append(unstack_heads(o2, GRID_W))
            o_ref[0, :, ln] = jnp.concatenate(outs, axis=0)


def _na_attention(q, k, v, bias_tabs, layer):
    nb, s_len, _ = q.shape
    ns = s_len // TILE
    n_rows = (s_len - TILE) // GRID_W
    lanes = 128 * NA_PAIRS
    return pl.pallas_call(
        functools.partial(_na_kernel, n_rows=n_rows),
        grid=(nb, NA_W // lanes, ns),
        in_specs=[pl.BlockSpec((1, TILE, lanes), lambda b, hp, s: (b, s, hp)),
                  pl.BlockSpec((1, s_len, lanes), lambda b, hp, s: (b, 0, hp)),
                  pl.BlockSpec((1, s_len, lanes), lambda b, hp, s: (b, 0, hp)),
                  pl.BlockSpec((None, 2 * NA_PAIRS, NA_KH, GRID_W, NA_KH * GRID_W),
                               lambda b, hp, s: (layer, hp, 0, 0, 0))],
        out_specs=pl.BlockSpec((1, TILE, lanes), lambda b, hp, s: (b, s, hp)),
        out_shape=jax.ShapeDtypeStruct((nb, s_len, NA_W), BF16),
        compiler_params=_cparams(3),
        name="na_attn",
    )(q, k, v, bias_tabs)


def _na_bias_tables(rpb):
    n_l, n_h, n_dr, n_dc = rpb.shape
    period = 2 * GRID_W
    left = (GRID_W - 1) - (NA_KW - 1)
    e = jnp.pad(rpb.astype(F32), ((0, 0), (0, 0), (0, 0), (left, period - n_dc - left)))
    skew = jnp.tile(e, (1, 1, 1, GRID_W))[..., :GRID_W * (period - 1)]
    skew = skew.reshape(n_l, n_h, n_dr, GRID_W, period - 1)[..., GRID_W - 1:]
    qi = np.arange(GRID_W)[:, None]
    kc = np.arange(GRID_W)[None, :]
    cs = np.clip(qi - NA_KW // 2, 0, GRID_W - NA_KW)
    in_win = np.logical_and(kc >= cs, kc < cs + NA_KW)
    toe = jnp.where(in_win, skew, MASK_NEG)
    toe = jnp.transpose(toe, (0, 1, 3, 2, 4))
    tab = jnp.stack([toe[:, :, :, c:c + NA_KH, :] for c in range(NA_KH)], axis=2)
    return tab.reshape(n_l, n_h, NA_KH, GRID_W, NA_KH * GRID_W)


def _post_mixer_kernel(*refs):
    mod_ref, ona_ref, of_ref, ob_ref, z_ref, gg_ref, gp_ref, w_ref, ones_ref, o_ref = refs[-10:]
    x = _stream_tile(refs[:-10])
    d = x.shape[-1]
    gate = mod_ref[0][:, 2 * d:3 * d]
    ones_bd = ones_ref[...]
    blocks = [slice(i * POST_ROWS, (i + 1) * POST_ROWS) for i in range(TILE // POST_ROWS)]
    ogs, mss = [], []
    for rows in blocks:
        og = of_ref[0, rows, :].astype(F32) + ob_ref[0, rows, :].astype(F32)
        ogs.append(og)
        mss.append(_dot((og * og).astype(BF16), ones_bd) * (1.0 / HEAD_DIM))
    ys = []
    for rows, og, ms in zip(blocks, ogs, mss):
        gated = og * lax.rsqrt(ms + RMS_EPS) * gg_ref[...] * _silu(z_ref[0, rows, :].astype(F32))
        ys.append(_dot(ona_ref[0, rows, :], w_ref[0:NA_W, :])
                  + _dot(gated.astype(BF16), w_ref[NA_W:NA_W + GDN_W, :]))
    for rows, y in zip(blocks, ys):
        o_ref[0, rows, :] = x[rows, :] + gate * _rms(y, gp_ref[...])


def _post_mixer(stream, mod_all, o_na, o_f, o_b, z, gdn_gain, post_gain, w_out, ones_bd, layer):
    nb, s_len, _ = o_na.shape
    d = post_gain.shape[-1]
    x_specs, x_args = _stream_specs(stream, d)
    ns = s_len // TILE
    tok = lambda w: pl.BlockSpec((1, TILE, w), lambda b, s: (b, s, 0))
    return pl.pallas_call(
        _post_mixer_kernel,
        grid=(nb, ns),
        in_specs=x_specs + [_mod_spec(mod_all, layer, nb),
                            tok(NA_W), tok(GDN_W), tok(GDN_W), tok(GDN_W),
                            _const_spec(gdn_gain.shape), _const_spec(post_gain.shape),
                            _layer_spec(w_out, layer), _const_spec(ones_bd.shape)],
        out_specs=tok(d),
        out_shape=jax.ShapeDtypeStruct((nb, s_len, d), F32),
        input_output_aliases={} if isinstance(stream, tuple) else {0: 0},
        compiler_params=_cparams(2),
        name="post_mixer",
    )(*x_args, mod_all, o_na, o_f, o_b, z, gdn_gain, post_gain, w_out, ones_bd)


def _conv_ffn_kernel(x_ref, xp_ref, xn_ref, mod_ref, gpre_ref, gpost_ref, wu_ref, cw_ref, wd_ref,
                     o_ref, act_s, *, first_block):
    s = pl.program_id(1) + first_block
    ns = pl.num_programs(1) + first_block
    d = x_ref.shape[-1]
    d_ff = wd_ref.shape[0]
    m = mod_ref[0]
    shift, scale, gate = m[:, 3 * d:4 * d], m[:, 4 * d:5 * d], m[:, 5 * d:6 * d]
    x = x_ref[0]
    prev_ok = (s >= 2).astype(F32)
    next_ok = jnp.logical_and(s >= 1, s < ns - 1).astype(F32)

    def hidden(t):
        return _rms(t, gpre_ref[...]) * (1.0 + scale) + shift

    h = jnp.concatenate([hidden(xp_ref[0]) * prev_ok, hidden(x), hidden(xn_ref[0]) * next_ok],
                        axis=0).astype(BF16)
    n_steps = d_ff // FFN_COLS

    def up(c):
        return [_dot(h, wu_ref[:, off:off + FFN_COLS]) for off in (c * FFN_COLS, d_ff + c * FFN_COLS)]

    def activation(c, us):
        halves = []
        n_rows = TILE + 2 * HALO
        for off, u in zip((c * FFN_COLS, d_ff + c * FFN_COLS), us):
            u_m1 = pltpu.roll(u, 1, 0)[HALO:HALO + TILE]
            u_p1 = pltpu.roll(u, n_rows - 1, 0)[HALO:HALO + TILE]
            halves.append(cw_ref[0:1, off:off + FFN_COLS] * u_m1
                          + cw_ref[1:2, off:off + FFN_COLS] * u[HALO:HALO + TILE]
                          + cw_ref[2:3, off:off + FFN_COLS] * u_p1)
        return (_silu(halves[0]) * halves[1]).astype(BF16)

    u_next = up(0)
    for c in range(n_steps):
        u_cur = u_next
        if c + 1 < n_steps:
            u_next = up(c + 1)
        act_s[:, c * FFN_COLS:(c + 1) * FFN_COLS] = activation(c, u_cur)
    acc = _dot(act_s[...], wd_ref[...])
    o_ref[0] = x + gate * _rms(acc, gpost_ref[...])


def _conv_ffn(x_all, mod_all, pre_gain, post_gain, w_up, conv_w, w_down, layer, latent_only):
    nb, s_len, d = x_all.shape
    ns = s_len // TILE
    hpt = TILE // HALO
    nh = s_len // HALO
    first = 1 if latent_only else 0
    return pl.pallas_call(
        functools.partial(_conv_ffn_kernel, first_block=first),
        grid=(nb, ns - first),
        in_specs=[pl.BlockSpec((1, TILE, d), lambda b, s: (b, s + first, 0)),
                  pl.BlockSpec((1, HALO, d),
                               lambda b, s: (b, jnp.maximum((s + first) * hpt - 1, 0), 0)),
                  pl.BlockSpec((1, HALO, d),
                               lambda b, s: (b, jnp.minimum((s + first + 1) * hpt, nh - 1), 0)),
                  _mod_spec(mod_all, layer, nb, first),
                  _const_spec(pre_gain.shape), _const_spec(post_gain.shape),
                  _layer_spec(w_up, layer), _const_spec(conv_w.shape), _layer_spec(w_down, layer)],
        out_specs=pl.BlockSpec((1, TILE, d), lambda b, s: (b, s, 0)),
        out_shape=jax.ShapeDtypeStruct((nb, s_len - first * TILE, d), F32),
        scratch_shapes=[pltpu.VMEM((TILE, w_down.shape[1]), BF16)],
        compiler_params=_cparams(2),
        name="conv_ffn",
    )(x_all, x_all, x_all, mod_all, pre_gain, post_gain, w_up, conv_w, w_down)


def _rotary_tables(n_ctx, n_lat):
    t = jnp.arange(n_lat)
    row = (t // GRID_W).astype(F32)
    col = (t % GRID_W).astype(F32)
    pairs = HEAD_DIM // 4
    inv_freq = ROPE_BASE ** (-jnp.arange(pairs, dtype=F32) / pairs)
    ang = jnp.concatenate([row[:, None] * inv_freq, col[:, None] * inv_freq], axis=-1)
    cos, sin = jnp.cos(ang), jnp.sin(ang)
    cos64 = jnp.concatenate([cos, cos], axis=-1)
    sin64 = jnp.concatenate([-sin, sin], axis=-1)
    cos_t = jnp.concatenate([jnp.ones((n_ctx, HEAD_DIM), F32), cos64], axis=0)
    sin_t = jnp.concatenate([jnp.zeros((n_ctx, HEAD_DIM), F32), sin64], axis=0)
    return jnp.tile(cos_t, (1, 2)), jnp.tile(sin_t, (1, 2))


def _static_masks():
    lane_h = np.arange(GROUP_W) // HEAD_DIM
    bdmask = (lane_h[:, None] == lane_h[None, :]).astype(np.float32)
    seg = np.arange(GDN_W) // HEAD_DIM
    ones_bd = (seg[:, None] == seg[None, :]).astype(np.float32)
    i = np.arange(CHUNK)[:, None]
    j = (np.arange(GROUP_W) % HEAD_DIM)[None, :]
    lvl = []
    for lv in range(N_LEVELS):
        n = 2 ** lv
        lvl.append(np.logical_and(i // (2 * n) == j // (2 * n), i // n != j // n))
    lvl.append(i == j)
    lvl = np.stack(lvl).astype(np.float32)
    lvlbd = np.tile(lvl[:N_LEVELS], (1, HEADS_PER_GROUP, 1)) * bdmask[None]
    return (jnp.asarray(bdmask, BF16), jnp.asarray(ones_bd, BF16), jnp.asarray(lvl, F32),
            jnp.asarray(lvlbd, BF16))


def _gate_lane_vector(p):
    flat = p.reshape(1, 2 * GDN_HEADS).astype(F32)
    return jnp.pad(flat, ((0, 0), (2 * GDN_HEADS, GATE_PAD - 4 * GDN_HEADS)))


def kernel(x, c, ctx, c_ctx, ada_w, ada_b, norm_mix_pre, norm_mix_post, w_in, qkv_conv, a_log,
           dt_bias, gdn_norm, rpb, w_out, norm_ffn_pre, norm_ffn_post, ffn_up, ffn_conv, ffn_down):
    nb, n_lat, d = x.shape
    n_ctx = ctx.shape[1]
    depth = ada_w.shape[0]
    assert n_ctx == TILE and n_lat % TILE == 0 and n_lat // GRID_W >= NA_KH

    mod_rows = -(-(nb + 1) // 8) * 8
    cc = jnp.concatenate([c, c_ctx[None, :], jnp.zeros((mod_rows - nb - 1, d), F32)], axis=0)
    mod_all = _modulation(cc, ada_w, ada_b).reshape(depth, mod_rows, 1, N_MOD * d)

    cos_t, sin_t = _rotary_tables(n_ctx, n_lat)
    bdmask, ones_bd, lvl, lvlbd = _static_masks()
    in_w = w_in.shape[-1]
    w_in_p = jnp.pad(w_in, ((0, 0), (0, 0), (0, GATE_PAD - (in_w - 3 * NA_W - 4 * GDN_W)))).astype(BF16)
    w_out_b = w_out.astype(BF16)
    w_up_b = ffn_up.astype(BF16)
    w_down_b = ffn_down.astype(BF16)
    bias_tabs = _na_bias_tables(rpb)

    s_len = n_ctx + n_lat
    stream = (ctx, x)
    for l in range(depth):
        naq, nak, nav, gqkv, z, gates = _pre_mixer(stream, s_len, mod_all, norm_mix_pre[l][None, :],
                                                   w_in_p, l)
        u, w, qg, qk, kdt, eg = _gdn_local(gqkv, gates, qkv_conv[l], _gate_lane_vector(a_log[l]),
                                           _gate_lane_vector(dt_bias[l]), cos_t, sin_t, ones_bd,
                                           bdmask, lvl, lvlbd)
        o_f, o_b = _gdn_scan(u, w, qg, qk, kdt, eg, bdmask)
        o_na = _na_attention(naq, nak, nav, bias_tabs, l)
        stream = _post_mixer(stream, mod_all, o_na, o_f, o_b, z, jnp.tile(gdn_norm[l], GDN_HEADS)[None, :],
                             norm_mix_post[l][None, :], w_out_b, ones_bd, l)
        stream = _conv_ffn(stream, mod_all, norm_ffn_pre[l][None, :], norm_ffn_post[l][None, :],
                           w_up_b, ffn_conv[l], w_down_b, l, latent_only=(l == depth - 1))
    return stream
```

```python
import functools

import jax
import jax.numpy as jnp
import numpy as np
from jax import lax
from jax.experimental import pallas as pl
from jax.experimental.pallas import tpu as pltpu

F32 = jnp.float32
BF16 = jnp.bfloat16

HEAD_DIM = 64
NA_HEADS = 8
GDN_HEADS = 8
NA_W = NA_HEADS * HEAD_DIM
GDN_W = GDN_HEADS * HEAD_DIM
GRID_W = 64
NA_KH = 8
NA_KW = 16
CHUNK = 64
ROPE_BASE = 10000.0
RMS_EPS = 1e-6
N_MOD = 6
TILE = 256
CHUNKS_PER_TILE = TILE // CHUNK
HALO = 8
GROUP_W = 128
HEADS_PER_GROUP = GROUP_W // HEAD_DIM
N_LEVELS = 6
LOCAL_PAR_CHUNKS = 2
NA_PAIRS = 2
POST_ROWS = 128
SCAN_BATCH = 2
MASK_NEG = -1e30
GATE_PAD = 128
FFN_COLS = 256
VMEM_LIMIT = 56 * 1024 * 1024


def _cparams(n_axes):
    return pltpu.CompilerParams(dimension_semantics=("arbitrary",) * n_axes,
                                vmem_limit_bytes=VMEM_LIMIT)


def _const_spec(shape):
    nd = len(shape)
    return pl.BlockSpec(shape, lambda *_: (0,) * nd)


def _layer_spec(arr, layer):
    nd = arr.ndim
    return pl.BlockSpec((None,) + arr.shape[1:], lambda *_: (layer,) + (0,) * (nd - 1))


def _mod_spec(mod_all, layer, n_batch, first=0):
    return pl.BlockSpec((None, 1, 1, mod_all.shape[-1]),
                        lambda b, s: (layer, _mod_row(b, s + first, n_batch), 0, 0))


def _stream_specs(stream, d):
    if isinstance(stream, tuple):
        ctx, lat = stream
        return ([pl.BlockSpec((1, TILE, d), lambda b, s: (b, 0, 0)),
                 pl.BlockSpec((1, TILE, d), lambda b, s: (b, jnp.maximum(s - 1, 0), 0))], [ctx, lat])
    return [pl.BlockSpec((1, TILE, d), lambda b, s: (b, s, 0))], [stream]


def _stream_tile(refs):
    if len(refs) == 2:
        return jnp.where(pl.program_id(1) == 0, refs[0][0], refs[1][0])
    return refs[0][0]


def _sigmoid(x):
    return 1.0 / (1.0 + jnp.exp(-x))


def _silu(x):
    return x * _sigmoid(x)


def _softplus(x):
    return jnp.maximum(x, 0.0) + jnp.log(1.0 + jnp.exp(-jnp.abs(x)))


def _rms(x, gain):
    ms = jnp.mean(x * x, axis=-1, keepdims=True)
    return x * lax.rsqrt(ms + RMS_EPS) * gain


def _dot(a, b):
    return jnp.dot(a, b, preferred_element_type=F32)


def _dot_nt(a, b):
    return lax.dot_general(a, b, (((1,), (1,)), ((), ())), preferred_element_type=F32)


def _mod_row(b, s, n_batch):
    return jnp.where(s == 0, n_batch, b)


def _modulation_kernel(c_ref, w_ref, b_ref, o_ref):
    a = _silu(c_ref[...]).astype(BF16)
    o_ref[0] = _dot(a, w_ref[0].astype(BF16)) + b_ref[0]


def _modulation(cc, ada_w, ada_b):
    depth, d, n = ada_w.shape
    rows = cc.shape[0]
    nblk = n // N_MOD
    return pl.pallas_call(
        _modulation_kernel,
        grid=(depth, N_MOD),
        in_specs=[pl.BlockSpec((rows, d), lambda l, j: (0, 0)),
                  pl.BlockSpec((1, d, nblk), lambda l, j: (l, 0, j)),
                  pl.BlockSpec((1, 1, nblk), lambda l, j: (l, 0, j))],
        out_specs=pl.BlockSpec((1, rows, nblk), lambda l, j: (l, 0, j)),
        out_shape=jax.ShapeDtypeStruct((depth, rows, n), F32),
        compiler_params=_cparams(2),
        name="modulation",
    )(cc, ada_w, ada_b.reshape(depth, 1, n))


def _pre_mixer_kernel(*refs):
    mod_ref, g_ref, w_ref, q_ref, k_ref, v_ref, gq_ref, z_ref, gt_ref = refs[-9:]
    x = _stream_tile(refs[:-9])
    d = x.shape[-1]
    m = mod_ref[0]
    h = _rms(x, g_ref[...]) * (1.0 + m[:, d:2 * d]) + m[:, 0:d]
    hb = h.astype(BF16)
    o = 0
    q_ref[0] = (_dot(hb, w_ref[:, o:o + NA_W]) * HEAD_DIM ** -0.5).astype(BF16)
    o += NA_W
    k_ref[0] = _dot(hb, w_ref[:, o:o + NA_W]).astype(BF16)
    o += NA_W
    v_ref[0] = _dot(hb, w_ref[:, o:o + NA_W]).astype(BF16)
    o += NA_W
    gq_ref[0] = _dot(hb, w_ref[:, o:o + 3 * GDN_W]).astype(BF16)
    o += 3 * GDN_W
    z_ref[0] = _dot(hb, w_ref[:, o:o + GDN_W]).astype(BF16)
    o += GDN_W
    gt_ref[0] = _dot(hb, w_ref[:, o:o + GATE_PAD])


def _pre_mixer(stream, s_len, mod_all, gain, w_in_p, layer):
    x_specs, x_args = _stream_specs(stream, gain.shape[-1])
    nb, d = x_args[0].shape[0], gain.shape[-1]
    ns = s_len // TILE
    tok = lambda w: pl.BlockSpec((1, TILE, w), lambda b, s: (b, s, 0))
    shp = lambda w, dt: jax.ShapeDtypeStruct((nb, s_len, w), dt)
    return pl.pallas_call(
        _pre_mixer_kernel,
        grid=(nb, ns),
        in_specs=x_specs + [_mod_spec(mod_all, layer, nb),
                            _const_spec((1, d)),
                            _layer_spec(w_in_p, layer)],
        out_specs=[tok(NA_W), tok(NA_W), tok(NA_W), tok(3 * GDN_W), tok(GDN_W), tok(GATE_PAD)],
        out_shape=[shp(NA_W, BF16), shp(NA_W, BF16), shp(NA_W, BF16), shp(3 * GDN_W, BF16),
                   shp(GDN_W, BF16), shp(GATE_PAD, F32)],
        compiler_params=_cparams(2),
        name="pre_mixer",
    )(*x_args, mod_all, gain, w_in_p)


def _block_diag(x, bdmask):
    return jnp.concatenate([x] * HEADS_PER_GROUP, axis=0) * bdmask


def _split3(x):
    a = x.astype(BF16)
    r = x - a.astype(F32)
    b = r.astype(BF16)
    c = (r - b.astype(F32)).astype(BF16)
    return a, b, c


def _expand_heads(a, c0):
    rows = a.shape[0]
    lane = lax.broadcasted_iota(jnp.int32, (rows, 128), 1)
    parts = []
    for p in range(GDN_HEADS // 2):
        lo = jnp.broadcast_to(a[:, c0 + 2 * p:c0 + 2 * p + 1], (rows, 128))
        hi = jnp.broadcast_to(a[:, c0 + 2 * p + 1:c0 + 2 * p + 2], (rows, 128))
        parts.append(jnp.where(lane < HEAD_DIM, lo, hi))
    return jnp.concatenate(parts, axis=1)


def _gdn_local_kernel(x_ref, xp_ref, xn_ref, gt_ref, cw_ref, av_ref, dt_ref, cos_ref, sin_ref,
                      ones_ref, bd_ref, lvl_ref, lvlbd_ref,
                      u_ref, w_ref, qg_ref, qk_ref, kdt_ref, eg_ref,
                      q_s, k_s, v_s, gc_s, be_s):
    s = pl.program_id(1)
    ns = pl.num_programs(1)
    width = x_ref.shape[-1]

    x = x_ref[0].astype(F32)
    prev_ok = (s >= 2).astype(F32)
    next_ok = jnp.logical_and(s >= 1, s < ns - 1).astype(F32)
    prev_row = xp_ref[0, HALO - 1:HALO, :].astype(F32) * prev_ok
    next_row = xn_ref[0, 0:1, :].astype(F32) * next_ok
    ri = lax.broadcasted_iota(jnp.int32, (TILE, width), 0)
    x_m1 = jnp.where(ri == 0, prev_row, pltpu.roll(x, 1, 0))
    x_p1 = jnp.where(ri == TILE - 1, next_row, pltpu.roll(x, TILE - 1, 0))
    y = cw_ref[0:1, :] * x_m1 + cw_ref[1:2, :] * x + cw_ref[2:3, :] * x_p1
    a = _silu(y)
    q = a[:, 0:GDN_W]
    k = a[:, GDN_W:2 * GDN_W]
    v_s[...] = a[:, 2 * GDN_W:3 * GDN_W]

    ones_bd = ones_ref[...]

    def seg_sum(t):
        return _dot(t.astype(BF16), ones_bd)

    q = q * lax.rsqrt(seg_sum(q * q) + RMS_EPS)
    k = k * lax.rsqrt(seg_sum(k * k) + RMS_EPS)
    cos_t = cos_ref[...]
    sin_t = sin_ref[...]
    lane = lax.broadcasted_iota(jnp.int32, (TILE, 128), 1)
    first_half = (lane % HEAD_DIM) < HEAD_DIM // 2

    def rope(t):
        parts = []
        for p in range(GDN_W // 128):
            ts = t[:, 128 * p:128 * (p + 1)]
            partner = jnp.where(first_half, pltpu.roll(ts, 128 - HEAD_DIM // 2, 1),
                                pltpu.roll(ts, HEAD_DIM // 2, 1))
            parts.append(ts * cos_t + partner * sin_t)
        return jnp.concatenate(parts, axis=1)

    q_s[...] = rope(q) * HEAD_DIM ** -0.5
    k_s[...] = rope(k)

    gt = gt_ref[0]
    beta = _sigmoid(gt)
    g_raw = -jnp.exp(av_ref[...]) * _softplus(gt + dt_ref[...])
    r_i = lax.broadcasted_iota(jnp.int32, (TILE, TILE), 0)
    c_i = lax.broadcasted_iota(jnp.int32, (TILE, TILE), 1)
    same_chunk = (r_i // CHUNK) == (c_i // CHUNK)
    tri_f = jnp.where(jnp.logical_and(same_chunk, r_i >= c_i), 1.0, 0.0).astype(BF16)
    tri_b = jnp.where(jnp.logical_and(same_chunk, r_i <= c_i), 1.0, 0.0).astype(BF16)
    g1, g2, g3 = _split3(g_raw)
    cum_f = _dot(tri_f, g1) + _dot(tri_f, g2) + _dot(tri_f, g3)
    cum_b = _dot(tri_b, g1) + _dot(tri_b, g2) + _dot(tri_b, g3)
    gc_s[0] = _expand_heads(cum_f, 2 * GDN_HEADS)
    gc_s[1] = _expand_heads(cum_b, 3 * GDN_HEADS)
    be_s[0] = _expand_heads(beta, 0)
    be_s[1] = _expand_heads(beta, GDN_HEADS)

    bdmask = bd_ref[...]
    eye = lvl_ref[N_LEVELS]
    ii = lax.broadcasted_iota(jnp.int32, (CHUNK, GROUP_W), 0)
    jj = lax.broadcasted_iota(jnp.int32, (CHUNK, GROUP_W), 1) % HEAD_DIM

    def setup(c, d, gi):
        rows = slice(c * CHUNK, (c + 1) * CHUNK)
        lanes = slice(gi * GROUP_W, (gi + 1) * GROUP_W)
        kk = k_s[rows, lanes]
        qq = q_s[rows, lanes]
        gc = gc_s[d, rows, lanes]
        be = be_s[d, rows, lanes]
        tri = (ii >= jj) if d == 0 else (ii <= jj)
        g_last = gc[CHUNK - 1:CHUNK, :] if d == 0 else gc[0:1, :]
        kb = kk * be
        e_gc = jnp.exp(gc)
        gc_t = jnp.sum(gc * eye, axis=0, keepdims=True)
        dec = jnp.exp(jnp.where(tri, gc - gc_t, MASK_NEG))
        qg_ref[d, 0, rows, lanes] = (qq * e_gc).astype(BF16)
        return dict(c=c, rows=rows, lanes=lanes, d=d, kk=kk, dec=dec, g_last=g_last,
                    lhs=jnp.concatenate([kb, qq], axis=0).astype(BF16),
                    vb=(v_s[rows, lanes] * be).astype(BF16), kbg=(kb * e_gc).astype(BF16),
                    kd=(kk * jnp.exp(g_last - gc)).astype(BF16))

    for c0 in range(0, CHUNKS_PER_TILE, LOCAL_PAR_CHUNKS):
        ps = [setup(c, d, gi) for c in range(c0, c0 + LOCAL_PAR_CHUNKS) for d in range(2)
              for gi in range(GDN_W // GROUP_W)]
        grams = [_dot_nt(p["lhs"], _block_diag(p["kk"].astype(BF16), bdmask)) for p in ps]
        kdts = [_dot_nt(eye.astype(BF16), _block_diag(p["kd"], bdmask)) for p in ps]
        ms = []
        for p, gram in zip(ps, grams):
            ms.append(gram[0:CHUNK] * p["dec"])
            qk_ref[p["d"], 0, p["rows"], p["lanes"]] = (gram[CHUNK:2 * CHUNK] * p["dec"]).astype(BF16)
        xs = [eye - m * lvl_ref[0] for m in ms]
        xb = [x.astype(BF16) for x in xs]
        mt = [jnp.concatenate([m.astype(BF16)] * HEADS_PER_GROUP, axis=0) for m in ms]
        for lv in range(1, N_LEVELS):
            ys = [_dot(x, m * lvlbd_ref[lv]) for x, m in zip(xb, mt)]
            zs = [_dot(y.astype(BF16), _block_diag(x, bdmask)) for y, x in zip(ys, xb)]
            xs = [x - z for x, z in zip(xs, zs)]
            xb = [x.astype(BF16) for x in xs]
        uws = [_dot(x, jnp.concatenate([_block_diag(p["vb"], bdmask), _block_diag(p["kbg"], bdmask)],
                                       axis=1)) for x, p in zip(xb, ps)]
        us = [uw[:, 0:GROUP_W] for uw in uws]
        ws = [uw[:, GROUP_W:2 * GROUP_W] for uw in uws]
        for p, u, w, kdt in zip(ps, us, ws, kdts):
            u_ref[p["d"], 0, p["rows"], p["lanes"]] = u.astype(BF16)
            w_ref[p["d"], 0, p["rows"], p["lanes"]] = w.astype(BF16)
            kdt_ref[p["d"], 0, p["rows"], p["lanes"]] = kdt.astype(BF16)
            eg_ref[p["d"], 0, 0, p["c"]:p["c"] + 1, p["lanes"]] = jnp.exp(p["g_last"])


def _gdn_local(gqkv, gates, conv_w, avec, dtvec, cos_t, sin_t, ones_bd, bdmask, lvl, lvlbd):
    nb, s_len, width = gqkv.shape
    ns = s_len // TILE
    hpt = TILE // HALO
    nh = s_len // HALO
    dir_spec = pl.BlockSpec((2, 1, TILE, GDN_W), lambda b, s: (0, b, s, 0))
    dir_shape = jax.ShapeDtypeStruct((2, nb, s_len, GDN_W), BF16)
    return pl.pallas_call(
        _gdn_local_kernel,
        grid=(nb, ns),
        in_specs=[pl.BlockSpec((1, TILE, width), lambda b, s: (b, s, 0)),
                  pl.BlockSpec((1, HALO, width), lambda b, s: (b, jnp.maximum(s * hpt - 1, 0), 0)),
                  pl.BlockSpec((1, HALO, width),
                               lambda b, s: (b, jnp.minimum((s + 1) * hpt, nh - 1), 0)),
                  pl.BlockSpec((1, TILE, GATE_PAD), lambda b, s: (b, s, 0)),
                  _const_spec(conv_w.shape),
                  _const_spec(avec.shape),
                  _const_spec(dtvec.shape),
                  pl.BlockSpec((TILE, 128), lambda b, s: (s, 0)),
                  pl.BlockSpec((TILE, 128), lambda b, s: (s, 0)),
                  _const_spec(ones_bd.shape),
                  _const_spec(bdmask.shape),
                  _const_spec(lvl.shape),
                  _const_spec(lvlbd.shape)],
        out_specs=[dir_spec] * 5 + [pl.BlockSpec((2, 1, 1, CHUNKS_PER_TILE, GDN_W),
                                                 lambda b, s: (0, b, s, 0, 0))],
        out_shape=[dir_shape] * 5 + [jax.ShapeDtypeStruct((2, nb, ns, CHUNKS_PER_TILE, GDN_W), F32)],
        scratch_shapes=[pltpu.VMEM((TILE, GDN_W), F32), pltpu.VMEM((TILE, GDN_W), F32),
                        pltpu.VMEM((TILE, GDN_W), F32), pltpu.VMEM((2, TILE, GDN_W), F32),
                        pltpu.VMEM((2, TILE, GDN_W), F32)],
        compiler_params=_cparams(2),
        name="gdn_local",
    )(gqkv, gqkv, gqkv, gates, conv_w, avec, dtvec, cos_t, sin_t, ones_bd, bdmask, lvl, lvlbd)


def _gdn_scan_kernel(uf, wf, qgf, qkf, kdf, egf, ub, wb, qgb, qkb, kdb, egb, bd_ref,
                     of_ref, ob_ref, state):
    j = pl.program_id(1)

    @pl.when(j == 0)
    def _():
        state[...] = jnp.zeros_like(state)

    bdmask = bd_ref[...]
    ins = ((uf, wf, qgf, qkf, kdf, egf, of_ref), (ub, wb, qgb, qkb, kdb, egb, ob_ref))
    n_groups = GDN_W // GROUP_W
    chains = [(bi, d, gi) for bi in range(SCAN_BATCH) for d in range(2) for gi in range(n_groups)]
    sts = [state[i] for i in range(len(chains))]
    for ci in range(CHUNKS_PER_TILE):
        sl = []
        for bi, d, gi in chains:
            c = ci if d == 0 else CHUNKS_PER_TILE - 1 - ci
            sl.append((c, slice(c * CHUNK, (c + 1) * CHUNK), slice(gi * GROUP_W, (gi + 1) * GROUP_W)))
        r1s = []
        for (bi, d, gi), (c, rows, lanes), st in zip(chains, sl, sts):
            w_r, qg_r = ins[d][1], ins[d][2]
            lhs1 = jnp.concatenate([w_r[0, bi, rows, lanes], qg_r[0, bi, rows, lanes]], axis=0)
            r1s.append(_dot(lhs1, _block_diag(st.astype(BF16), bdmask)))
        r2s = []
        for (bi, d, gi), (c, rows, lanes), r1 in zip(chains, sl, r1s):
            u_r, qk_r, kd_r = ins[d][0], ins[d][3], ins[d][4]
            v_new = u_r[0, bi, rows, lanes].astype(F32) - r1[0:CHUNK]
            lhs2 = jnp.concatenate([qk_r[0, bi, rows, lanes], kd_r[0, bi, rows, lanes]], axis=0)
            r2s.append(_dot(lhs2, _block_diag(v_new.astype(BF16), bdmask)))
        new_sts = []
        for (bi, d, gi), (c, rows, lanes), st, r1, r2 in zip(chains, sl, sts, r1s, r2s):
            eg_r, o_r = ins[d][5], ins[d][6]
            o_r[bi, rows, lanes] = (r1[CHUNK:2 * CHUNK] + r2[0:CHUNK]).astype(BF16)
            new_sts.append(st * eg_r[0, bi, 0, c:c + 1, lanes] + r2[CHUNK:2 * CHUNK])
        sts = new_sts
    for i, st in enumerate(sts):
        state[i] = st


def _gdn_scan(u, w, qg, qk, kdt, eg, bdmask):
    _, nb, s_len, _ = u.shape
    ns = s_len // TILE
    sb = SCAN_BATCH
    assert nb % sb == 0
    bwd = lambda j: jnp.where(j == 0, 0, ns - j)
    f_spec = pl.BlockSpec((1, sb, TILE, GDN_W), lambda b, j: (0, b, j, 0))
    b_spec = pl.BlockSpec((1, sb, TILE, GDN_W), lambda b, j: (1, b, bwd(j), 0))
    egf_spec = pl.BlockSpec((1, sb, 1, CHUNKS_PER_TILE, GDN_W), lambda b, j: (0, b, j, 0, 0))
    egb_spec = pl.BlockSpec((1, sb, 1, CHUNKS_PER_TILE, GDN_W), lambda b, j: (1, b, bwd(j), 0, 0))
    out_shape = jax.ShapeDtypeStruct((nb, s_len, GDN_W), BF16)
    return pl.pallas_call(
        _gdn_scan_kernel,
        grid=(nb // sb, ns),
        in_specs=[f_spec] * 5 + [egf_spec] + [b_spec] * 5 + [egb_spec] + [_const_spec(bdmask.shape)],
        out_specs=[pl.BlockSpec((sb, TILE, GDN_W), lambda b, j: (b, j, 0)),
                   pl.BlockSpec((sb, TILE, GDN_W), lambda b, j: (b, bwd(j), 0))],
        out_shape=[out_shape, out_shape],
        scratch_shapes=[pltpu.VMEM((sb * 2 * GDN_W // GROUP_W, CHUNK, GROUP_W), F32)],
        compiler_params=_cparams(2),
        name="gdn_scan",
    )(u, w, qg, qk, kdt, eg, u, w, qg, qk, kdt, eg, bdmask)


def _na_kernel(q_ref, k_ref, v_ref, bias_ref, o_ref, *, n_rows):
    s = pl.program_id(2)
    lane = lax.broadcasted_iota(jnp.int32, (1, 128), 1)
    head_mask = (lane < HEAD_DIM, lane >= HEAD_DIM)
    pair_lanes = [slice(128 * p, 128 * (p + 1)) for p in range(NA_PAIRS)]

    def stack_heads(q):
        return jnp.concatenate([jnp.where(head_mask[h], q, jnp.zeros_like(q)) for h in range(2)],
                               axis=0)

    def unstack_heads(o2, n):
        return jnp.where(head_mask[0], o2[0:n], o2[n:2 * n]).astype(BF16)

    @pl.when(s == 0)
    def _():
        q2s = [stack_heads(q_ref[0, :, ln]) for ln in pair_lanes]
        scs = [_dot_nt(q2, k_ref[0, 0:TILE, ln]) for q2, ln in zip(q2s, pair_lanes)]
        ps, dens = [], []
        for sc in scs:
            p = jnp.exp(sc - jnp.max(sc, axis=-1, keepdims=True))
            dens.append(jnp.sum(p, axis=-1, keepdims=True))
            ps.append(p.astype(BF16))
        os_ = [_dot(p, v_ref[0, 0:TILE, ln]) for p, ln in zip(ps, pair_lanes)]
        for o2, den, ln in zip(os_, dens, pair_lanes):
            o_ref[0, :, ln] = unstack_heads(o2 / den, TILE)

    @pl.when(s > 0)
    def _():
        rows_per_tile = TILE // GRID_W
        probs = []
        for p, ln in enumerate(pair_lanes):
            for r in range(rows_per_tile):
                row = (s - 1) * rows_per_tile + r
                rs = jnp.clip(row - NA_KH // 2, 0, n_rows - NA_KH)
                cls = rs - row + (NA_KH - 1)
                start = pl.multiple_of(TILE + rs * GRID_W, GRID_W)
                probs.append(dict(
                    p=p, r=r, ln=ln,
                    kw=k_ref[0, pl.ds(start, NA_KH * GRID_W), ln],
                    vw=v_ref[0, pl.ds(start, NA_KH * GRID_W), ln],
                    bias=jnp.concatenate([bias_ref[2 * p, cls], bias_ref[2 * p + 1, cls]], axis=0),
                    q2=stack_heads(q_ref[0, r * GRID_W:(r + 1) * GRID_W, ln])))
        sc_alls = [_dot_nt(jnp.concatenate([pb["q2"] for pb in probs if pb["p"] == p], axis=0),
                           k_ref[0, 0:TILE, ln]) for p, ln in enumerate(pair_lanes)]
        sws = [_dot_nt(pb["q2"], pb["kw"]) + pb["bias"] for pb in probs]
        pws, pcs, dens = [], [], []
        for pb, sw in zip(probs, sws):
            r = pb["r"]
            sc = sc_alls[pb["p"]][2 * GRID_W * r:2 * GRID_W * (r + 1)]
            mx = jnp.maximum(jnp.max(sw, axis=-1, keepdims=True), jnp.max(sc, axis=-1, keepdims=True))
            pw = jnp.exp(sw - mx)
            pc = jnp.exp(sc - mx)
            dens.append(jnp.sum(pw, axis=-1, keepdims=True) + jnp.sum(pc, axis=-1, keepdims=True))
            pws.append(pw.astype(BF16))
            pcs.append(pc.astype(BF16))
        oc_alls = [_dot(jnp.concatenate([pc for pb, pc in zip(probs, pcs) if pb["p"] == p], axis=0),
                        v_ref[0, 0:TILE, ln]) for p, ln in enumerate(pair_lanes)]
        ows = [_dot(pw, pb["vw"]) for pb, pw in zip(probs, pws)]
        for p, ln in enumerate(pair_lanes):
            outs = []
            for pb, ow, den in zip(probs, ows, dens):
                if pb["p"] == p:
                    r = pb["r"]
                    o2 = (ow + oc_alls[p][2 * GRID_W * r:2 * GRID_W * (r + 1)]) / den
                    outs.append(unstack_heads(o2, GRID_W))
            o_ref[0, :, ln] = jnp.concatenate(outs, axis=0)


def _na_attention(q, k, v, bias_tabs, layer):
    nb, s_len, _ = q.shape
    ns = s_len // TILE
    n_rows = (s_len - TILE) // GRID_W
    lanes = 128 * NA_PAIRS
    return pl.pallas_call(
        functools.partial(_na_kernel, n_rows=n_rows),
        grid=(nb, NA_W // lanes, ns),
        in_specs=[pl.BlockSpec((1, TILE, lanes), lambda b, hp, s: (b, s, hp)),
                  pl.BlockSpec((1, s_len, lanes), lambda b, hp, s: (b, 0, hp)),
                  pl.BlockSpec((1, s_len, lanes), lambda b, hp, s: (b, 0, hp)),
                  pl.BlockSpec((None, 2 * NA_PAIRS, NA_KH, GRID_W, NA_KH * GRID_W),
                               lambda b, hp, s: (layer, hp, 0, 0, 0))],
        out_specs=pl.BlockSpec((1, TILE, lanes), lambda b, hp, s: (b, s, hp)),
        out_shape=jax.ShapeDtypeStruct((nb, s_len, NA_W), BF16),
        compiler_params=_cparams(3),
        name="na_attn",
    )(q, k, v, bias_tabs)


def _na_bias_tables(rpb):
    n_l, n_h, n_dr, n_dc = rpb.shape
    period = 2 * GRID_W
    left = (GRID_W - 1) - (NA_KW - 1)
    e = jnp.pad(rpb.astype(F32), ((0, 0), (0, 0), (0, 0), (left, period - n_dc - left)))
    skew = jnp.tile(e, (1, 1, 1, GRID_W))[..., :GRID_W * (period - 1)]
    skew = skew.reshape(n_l, n_h, n_dr, GRID_W, period - 1)[..., GRID_W - 1:]
    qi = np.arange(GRID_W)[:, None]
    kc = np.arange(GRID_W)[None, :]
    cs = np.clip(qi - NA_KW // 2, 0, GRID_W - NA_KW)
    in_win = np.logical_and(kc >= cs, kc < cs + NA_KW)
    toe = jnp.where(in_win, skew, MASK_NEG)
    blk = GRID_W * GRID_W
    flat = jnp.pad(toe.reshape(n_l, n_h, n_dr * blk), ((0, 0), (0, 0), (0, blk)))
    tab = jnp.tile(flat, (1, 1, NA_KH + 1))[..., :NA_KH * (n_dr + 2) * blk]
    tab = tab.reshape(n_l, n_h, NA_KH, n_dr + 2, GRID_W, GRID_W)[:, :, :, :NA_KH]
    tab = jnp.transpose(tab, (0, 1, 2, 4, 3, 5))
    return tab.reshape(n_l, n_h, NA_KH, GRID_W, NA_KH * GRID_W)


def _post_mixer_kernel(*refs):
    mod_ref, ona_ref, of_ref, ob_ref, z_ref, gg_ref, gp_ref, w_ref, ones_ref, o_ref = refs[-10:]
    x = _stream_tile(refs[:-10])
    d = x.shape[-1]
    gate = mod_ref[0][:, 2 * d:3 * d]
    ones_bd = ones_ref[...]
    blocks = [slice(i * POST_ROWS, (i + 1) * POST_ROWS) for i in range(TILE // POST_ROWS)]
    ogs, mss = [], []
    for rows in blocks:
        og = of_ref[0, rows, :].astype(F32) + ob_ref[0, rows, :].astype(F32)
        ogs.append(og)
        mss.append(_dot((og * og).astype(BF16), ones_bd) * (1.0 / HEAD_DIM))
    ys = []
    for rows, og, ms in zip(blocks, ogs, mss):
        gated = og * lax.rsqrt(ms + RMS_EPS) * gg_ref[...] * _silu(z_ref[0, rows, :].astype(F32))
        ys.append(_dot(ona_ref[0, rows, :], w_ref[0:NA_W, :])
                  + _dot(gated.astype(BF16), w_ref[NA_W:NA_W + GDN_W, :]))
    for rows, y in zip(blocks, ys):
        o_ref[0, rows, :] = x[rows, :] + gate * _rms(y, gp_ref[...])


def _post_mixer(stream, mod_all, o_na, o_f, o_b, z, gdn_gain, post_gain, w_out, ones_bd, layer):
    nb, s_len, _ = o_na.shape
    d = post_gain.shape[-1]
    x_specs, x_args = _stream_specs(stream, d)
    ns = s_len // TILE
    tok = lambda w: pl.BlockSpec((1, TILE, w), lambda b, s: (b, s, 0))
    return pl.pallas_call(
        _post_mixer_kernel,
        grid=(nb, ns),
        in_specs=x_specs + [_mod_spec(mod_all, layer, nb),
                            tok(NA_W), tok(GDN_W), tok(GDN_W), tok(GDN_W),
                            _const_spec(gdn_gain.shape), _const_spec(post_gain.shape),
                            _layer_spec(w_out, layer), _const_spec(ones_bd.shape)],
        out_specs=tok(d),
        out_shape=jax.ShapeDtypeStruct((nb, s_len, d), F32),
        input_output_aliases={} if isinstance(stream, tuple) else {0: 0},
        compiler_params=_cparams(2),
        name="post_mixer",
    )(*x_args, mod_all, o_na, o_f, o_b, z, gdn_gain, post_gain, w_out, ones_bd)


def _conv_ffn_kernel(x_ref, xp_ref, xn_ref, mod_ref, gpre_ref, gpost_ref, wu_ref, cw_ref, wd_ref,
                     o_ref, act_s, *, first_block):
    s = pl.program_id(1) + first_block
    ns = pl.num_programs(1) + first_block
    d = x_ref.shape[-1]
    d_ff = wd_ref.shape[0]
    m = mod_ref[0]
    shift, scale, gate = m[:, 3 * d:4 * d], m[:, 4 * d:5 * d], m[:, 5 * d:6 * d]
    x = x_ref[0]
    prev_ok = (s >= 2).astype(F32)
    next_ok = jnp.logical_and(s >= 1, s < ns - 1).astype(F32)

    def hidden(t):
        return _rms(t, gpre_ref[...]) * (1.0 + scale) + shift

    h = jnp.concatenate([hidden(xp_ref[0]) * prev_ok, hidden(x), hidden(xn_ref[0]) * next_ok],
                        axis=0).astype(BF16)
    n_steps = d_ff // FFN_COLS

    def up(c):
        return [_dot(h, wu_ref[:, off:off + FFN_COLS]) for off in (c * FFN_COLS, d_ff + c * FFN_COLS)]

    def activation(c, us):
        halves = []
        n_rows = TILE + 2 * HALO
        for off, u in zip((c * FFN_COLS, d_ff + c * FFN_COLS), us):
            u_m1 = pltpu.roll(u, 1, 0)[HALO:HALO + TILE]
            u_p1 = pltpu.roll(u, n_rows - 1, 0)[HALO:HALO + TILE]
            halves.append(cw_ref[0:1, off:off + FFN_COLS] * u_m1
                          + cw_ref[1:2, off:off + FFN_COLS] * u[HALO:HALO + TILE]
                          + cw_ref[2:3, off:off + FFN_COLS] * u_p1)
        return (_silu(halves[0]) * halves[1]).astype(BF16)

    u_next = up(0)
    for c in range(n_steps):
        u_cur = u_next
        if c + 1 < n_steps:
            u_next = up(c + 1)
        act_s[:, c * FFN_COLS:(c + 1) * FFN_COLS] = activation(c, u_cur)
    acc = _dot(act_s[...], wd_ref[...])
    o_ref[0] = x + gate * _rms(acc, gpost_ref[...])


def _conv_ffn(x_all, mod_all, pre_gain, post_gain, w_up, conv_w, w_down, layer, latent_only):
    nb, s_len, d = x_all.shape
    ns = s_len // TILE
    hpt = TILE // HALO
    nh = s_len // HALO
    first = 1 if latent_only else 0
    return pl.pallas_call(
        functools.partial(_conv_ffn_kernel, first_block=first),
        grid=(nb, ns - first),
        in_specs=[pl.BlockSpec((1, TILE, d), lambda b, s: (b, s + first, 0)),
                  pl.BlockSpec((1, HALO, d),
                               lambda b, s: (b, jnp.maximum((s + first) * hpt - 1, 0), 0)),
                  pl.BlockSpec((1, HALO, d),
                               lambda b, s: (b, jnp.minimum((s + first + 1) * hpt, nh - 1), 0)),
                  _mod_spec(mod_all, layer, nb, first),
                  _const_spec(pre_gain.shape), _const_spec(post_gain.shape),
                  _layer_spec(w_up, layer), _const_spec(conv_w.shape), _layer_spec(w_down, layer)],
        out_specs=pl.BlockSpec((1, TILE, d), lambda b, s: (b, s, 0)),
        out_shape=jax.ShapeDtypeStruct((nb, s_len - first * TILE, d), F32),
        scratch_shapes=[pltpu.VMEM((TILE, w_down.shape[1]), BF16)],
        compiler_params=_cparams(2),
        name="conv_ffn",
    )(x_all, x_all, x_all, mod_all, pre_gain, post_gain, w_up, conv_w, w_down)


def _rotary_tables(n_ctx, n_lat):
    t = jnp.arange(n_lat)
    row = (t // GRID_W).astype(F32)
    col = (t % GRID_W).astype(F32)
    pairs = HEAD_DIM // 4
    inv_freq = ROPE_BASE ** (-jnp.arange(pairs, dtype=F32) / pairs)
    ang = jnp.concatenate([row[:, None] * inv_freq, col[:, None] * inv_freq], axis=-1)
    cos, sin = jnp.cos(ang), jnp.sin(ang)
    cos64 = jnp.concatenate([cos, cos], axis=-1)
    sin64 = jnp.concatenate([-sin, sin], axis=-1)
    cos_t = jnp.concatenate([jnp.ones((n_ctx, HEAD_DIM), F32), cos64], axis=0)
    sin_t = jnp.concatenate([jnp.zeros((n_ctx, HEAD_DIM), F32), sin64], axis=0)
    return jnp.tile(cos_t, (1, 2)), jnp.tile(sin_t, (1, 2))


def _static_masks():
    lane_h = np.arange(GROUP_W) // HEAD_DIM
    bdmask = (lane_h[:, None] == lane_h[None, :]).astype(np.float32)
    seg = np.arange(GDN_W) // HEAD_DIM
    ones_bd = (seg[:, None] == seg[None, :]).astype(np.float32)
    i = np.arange(CHUNK)[:, None]
    j = (np.arange(GROUP_W) % HEAD_DIM)[None, :]
    lvl = []
    for lv in range(N_LEVELS):
        n = 2 ** lv
        lvl.append(np.logical_and(i // (2 * n) == j // (2 * n), i // n != j // n))
    lvl.append(i == j)
    lvl = np.stack(lvl).astype(np.float32)
    lvlbd = np.tile(lvl[:N_LEVELS], (1, HEADS_PER_GROUP, 1)) * bdmask[None]
    return (jnp.asarray(bdmask, BF16), jnp.asarray(ones_bd, BF16), jnp.asarray(lvl, F32),
            jnp.asarray(lvlbd, BF16))


def _gate_lane_vector(p):
    flat = p.reshape(1, 2 * GDN_HEADS).astype(F32)
    return jnp.pad(flat, ((0, 0), (2 * GDN_HEADS, GATE_PAD - 4 * GDN_HEADS)))


def kernel(x, c, ctx, c_ctx, ada_w, ada_b, norm_mix_pre, norm_mix_post, w_in, qkv_conv, a_log,
           dt_bias, gdn_norm, rpb, w_out, norm_ffn_pre, norm_ffn_post, ffn_up, ffn_conv, ffn_down):
    nb, n_lat, d = x.shape
    n_ctx = ctx.shape[1]
    depth = ada_w.shape[0]
    assert n_ctx == TILE and n_lat % TILE == 0 and n_lat // GRID_W >= NA_KH

    mod_rows = -(-(nb + 1) // 8) * 8
    cc = jnp.concatenate([c, c_ctx[None, :], jnp.zeros((mod_rows - nb - 1, d), F32)], axis=0)
    mod_all = _modulation(cc, ada_w, ada_b).reshape(depth, mod_rows, 1, N_MOD * d)

    cos_t, sin_t = _rotary_tables(n_ctx, n_lat)
    bdmask, ones_bd, lvl, lvlbd = _static_masks()
    in_w = w_in.shape[-1]
    w_in_p = jnp.pad(w_in, ((0, 0), (0, 0), (0, GATE_PAD - (in_w - 3 * NA_W - 4 * GDN_W)))).astype(BF16)
    w_out_b = w_out.astype(BF16)
    w_up_b = ffn_up.astype(BF16)
    w_down_b = ffn_down.astype(BF16)
    bias_tabs = _na_bias_tables(rpb)

    s_len = n_ctx + n_lat
    stream = (ctx, x)
    for l in range(depth):
        naq, nak, nav, gqkv, z, gates = _pre_mixer(stream, s_len, mod_all, norm_mix_pre[l][None, :],
                                                   w_in_p, l)
        u, w, qg, qk, kdt, eg = _gdn_local(gqkv, gates, qkv_conv[l], _gate_lane_vector(a_log[l]),
                                           _gate_lane_vector(dt_bias[l]), cos_t, sin_t, ones_bd,
                                           bdmask, lvl, lvlbd)
        o_f, o_b = _gdn_scan(u, w, qg, qk, kdt, eg, bdmask)
        o_na = _na_attention(naq, nak, nav, bias_tabs, l)
        stream = _post_mixer(stream, mod_all, o_na, o_f, o_b, z, jnp.tile(gdn_norm[l], GDN_HEADS)[None, :],
                             norm_mix_post[l][None, :], w_out_b, ones_bd, l)
        stream = _conv_ffn(stream, mod_all, norm_ffn_pre[l][None, :], norm_ffn_post[l][None, :],
                           w_up_b, ffn_conv[l], w_down_b, l, latent_only=(l == depth - 1))
    return stream
```

```python
import functools

import jax
import jax.numpy as jnp
import numpy as np
from jax import lax
from jax.experimental import pallas as pl
from jax.experimental.pallas import tpu as pltpu

F32 = jnp.float32
BF16 = jnp.bfloat16

HEAD_DIM = 64
NA_HEADS = 8
GDN_HEADS = 8
NA_W = NA_HEADS * HEAD_DIM
GDN_W = GDN_HEADS * HEAD_DIM
GRID_W = 64
NA_KH = 8
NA_KW = 16
CHUNK = 64
ROPE_BASE = 10000.0
RMS_EPS = 1e-6
N_MOD = 6
TILE = 256
CHUNKS_PER_TILE = TILE // CHUNK
HALO = 8
GROUP_W = 128
HEADS_PER_GROUP = GROUP_W // HEAD_DIM
N_LEVELS = 6
LOCAL_PAR_CHUNKS = 2
NA_PAIRS = 2
POST_ROWS = 128
SCAN_BATCH = 2
MASK_NEG = -1e30
GATE_PAD = 128
FFN_COLS = 256
VMEM_LIMIT = 56 * 1024 * 1024


def _cparams(n_axes):
    return pltpu.CompilerParams(dimension_semantics=("arbitrary",) * n_axes,
                                vmem_limit_bytes=VMEM_LIMIT)


def _const_spec(shape):
    nd = len(shape)
    return pl.BlockSpec(shape, lambda *_: (0,) * nd)


def _layer_spec(arr, layer):
    nd = arr.ndim
    return pl.BlockSpec((None,) + arr.shape[1:], lambda *_: (layer,) + (0,) * (nd - 1))


def _mod_spec(mod_all, layer, n_batch, first=0):
    return pl.BlockSpec((None, 1, 1, mod_all.shape[-1]),
                        lambda b, s: (layer, _mod_row(b, s + first, n_batch), 0, 0))


def _stream_specs(stream, d):
    if isinstance(stream, tuple):
        ctx, lat = stream
        return ([pl.BlockSpec((1, TILE, d), lambda b, s: (b, 0, 0)),
                 pl.BlockSpec((1, TILE, d), lambda b, s: (b, jnp.maximum(s - 1, 0), 0))], [ctx, lat])
    return [pl.BlockSpec((1, TILE, d), lambda b, s: (b, s, 0))], [stream]


def _stream_tile(refs):
    if len(refs) == 2:
        return jnp.where(pl.program_id(1) == 0, refs[0][0], refs[1][0])
    return refs[0][0]


def _sigmoid(x):
    return 1.0 / (1.0 + jnp.exp(-x))


def _silu(x):
    return x * _sigmoid(x)


def _softplus(x):
    return jnp.maximum(x, 0.0) + jnp.log(1.0 + jnp.exp(-jnp.abs(x)))


def _rms(x, gain):
    ms = jnp.mean(x * x, axis=-1, keepdims=True)
    return x * lax.rsqrt(ms + RMS_EPS) * gain


def _dot(a, b):
    return jnp.dot(a, b, preferred_element_type=F32)


def _dot_nt(a, b):
    return lax.dot_general(a, b, (((1,), (1,)), ((), ())), preferred_element_type=F32)


def _mod_row(b, s, n_batch):
    return jnp.where(s == 0, n_batch, b)


def _modulation_kernel(c_ref, w_ref, b_ref, o_ref):
    a = _silu(c_ref[...]).astype(BF16)
    o_ref[0] = _dot(a, w_ref[0].astype(BF16)) + b_ref[0]


def _modulation(cc, ada_w, ada_b):
    depth, d, n = ada_w.shape
    rows = cc.shape[0]
    nblk = n // N_MOD
    return pl.pallas_call(
        _modulation_kernel,
        grid=(depth, N_MOD),
        in_specs=[pl.BlockSpec((rows, d), lambda l, j: (0, 0)),
                  pl.BlockSpec((1, d, nblk), lambda l, j: (l, 0, j)),
                  pl.BlockSpec((1, 1, nblk), lambda l, j: (l, 0, j))],
        out_specs=pl.BlockSpec((1, rows, nblk), lambda l, j: (l, 0, j)),
        out_shape=jax.ShapeDtypeStruct((depth, rows, n), F32),
        compiler_params=_cparams(2),
        name="modulation",
    )(cc, ada_w, ada_b.reshape(depth, 1, n))


def _pre_mixer_kernel(*refs):
    mod_ref, g_ref, w_ref, q_ref, k_ref, v_ref, gq_ref, z_ref, gt_ref = refs[-9:]
    x = _stream_tile(refs[:-9])
    d = x.shape[-1]
    m = mod_ref[0]
    h = _rms(x, g_ref[...]) * (1.0 + m[:, d:2 * d]) + m[:, 0:d]
    hb = h.astype(BF16)
    o = 0
    q_ref[0] = (_dot(hb, w_ref[:, o:o + NA_W]) * HEAD_DIM ** -0.5).astype(BF16)
    o += NA_W
    k_ref[0] = _dot(hb, w_ref[:, o:o + NA_W]).astype(BF16)
    o += NA_W
    v_ref[0] = _dot(hb, w_ref[:, o:o + NA_W]).astype(BF16)
    o += NA_W
    gq_ref[0] = _dot(hb, w_ref[:, o:o + 3 * GDN_W]).astype(BF16)
    o += 3 * GDN_W
    z_ref[0] = _dot(hb, w_ref[:, o:o + GDN_W]).astype(BF16)
    o += GDN_W
    gt_ref[0] = _dot(hb, w_ref[:, o:o + GATE_PAD])


def _pre_mixer(stream, s_len, mod_all, gain, w_in_p, layer):
    x_specs, x_args = _stream_specs(stream, gain.shape[-1])
    nb, d = x_args[0].shape[0], gain.shape[-1]
    ns = s_len // TILE
    tok = lambda w: pl.BlockSpec((1, TILE, w), lambda b, s: (b, s, 0))
    shp = lambda w, dt: jax.ShapeDtypeStruct((nb, s_len, w), dt)
    return pl.pallas_call(
        _pre_mixer_kernel,
        grid=(nb, ns),
        in_specs=x_specs + [_mod_spec(mod_all, layer, nb),
                            _const_spec((1, d)),
                            _layer_spec(w_in_p, layer)],
        out_specs=[tok(NA_W), tok(NA_W), tok(NA_W), tok(3 * GDN_W), tok(GDN_W), tok(GATE_PAD)],
        out_shape=[shp(NA_W, BF16), shp(NA_W, BF16), shp(NA_W, BF16), shp(3 * GDN_W, BF16),
                   shp(GDN_W, BF16), shp(GATE_PAD, F32)],
        compiler_params=_cparams(2),
        name="pre_mixer",
    )(*x_args, mod_all, gain, w_in_p)


def _block_diag(x, bdmask):
    return jnp.concatenate([x] * HEADS_PER_GROUP, axis=0) * bdmask


def _split3(x):
    a = x.astype(BF16)
    r = x - a.astype(F32)
    b = r.astype(BF16)
    c = (r - b.astype(F32)).astype(BF16)
    return a, b, c


def _expand_heads(a, c0):
    rows = a.shape[0]
    lane = lax.broadcasted_iota(jnp.int32, (rows, 128), 1)
    parts = []
    for p in range(GDN_HEADS // 2):
        lo = jnp.broadcast_to(a[:, c0 + 2 * p:c0 + 2 * p + 1], (rows, 128))
        hi = jnp.broadcast_to(a[:, c0 + 2 * p + 1:c0 + 2 * p + 2], (rows, 128))
        parts.append(jnp.where(lane < HEAD_DIM, lo, hi))
    return jnp.concatenate(parts, axis=1)


def _gdn_local_kernel(x_ref, xp_ref, xn_ref, gt_ref, cw_ref, av_ref, dt_ref, cos_ref, sin_ref,
                      ones_ref, bd_ref, lvl_ref, lvlbd_ref,
                      u_ref, w_ref, qg_ref, qk_ref, kdt_ref, eg_ref,
                      q_s, k_s, v_s, gc_s, be_s):
    s = pl.program_id(1)
    ns = pl.num_programs(1)
    width = x_ref.shape[-1]

    x = x_ref[0].astype(F32)
    prev_ok = (s >= 2).astype(F32)
    next_ok = jnp.logical_and(s >= 1, s < ns - 1).astype(F32)
    prev_row = xp_ref[0, HALO - 1:HALO, :].astype(F32) * prev_ok
    next_row = xn_ref[0, 0:1, :].astype(F32) * next_ok
    ri = lax.broadcasted_iota(jnp.int32, (TILE, width), 0)
    x_m1 = jnp.where(ri == 0, prev_row, pltpu.roll(x, 1, 0))
    x_p1 = jnp.where(ri == TILE - 1, next_row, pltpu.roll(x, TILE - 1, 0))
    y = cw_ref[0:1, :] * x_m1 + cw_ref[1:2, :] * x + cw_ref[2:3, :] * x_p1
    a = _silu(y)
    q = a[:, 0:GDN_W]
    k = a[:, GDN_W:2 * GDN_W]
    v_s[...] = a[:, 2 * GDN_W:3 * GDN_W]

    ones_bd = ones_ref[...]

    def seg_sum(t):
        return _dot(t.astype(BF16), ones_bd)

    q = q * lax.rsqrt(seg_sum(q * q) + RMS_EPS)
    k = k * lax.rsqrt(seg_sum(k * k) + RMS_EPS)
    cos_t = cos_ref[...]
    sin_t = sin_ref[...]
    lane = lax.broadcasted_iota(jnp.int32, (TILE, 128), 1)
    first_half = (lane % HEAD_DIM) < HEAD_DIM // 2

    def rope(t):
        parts = []
        for p in range(GDN_W // 128):
            ts = t[:, 128 * p:128 * (p + 1)]
            partner = jnp.where(first_half, pltpu.roll(ts, 128 - HEAD_DIM // 2, 1),
                                pltpu.roll(ts, HEAD_DIM // 2, 1))
            parts.append(ts * cos_t + partner * sin_t)
        return jnp.concatenate(parts, axis=1)

    q_s[...] = rope(q) * HEAD_DIM ** -0.5
    k_s[...] = rope(k)

    gt = gt_ref[0]
    beta = _sigmoid(gt)
    g_raw = -jnp.exp(av_ref[...]) * _softplus(gt + dt_ref[...])
    r_i = lax.broadcasted_iota(jnp.int32, (TILE, TILE), 0)
    c_i = lax.broadcasted_iota(jnp.int32, (TILE, TILE), 1)
    same_chunk = (r_i // CHUNK) == (c_i // CHUNK)
    tri_f = jnp.where(jnp.logical_and(same_chunk, r_i >= c_i), 1.0, 0.0).astype(BF16)
    tri_b = jnp.where(jnp.logical_and(same_chunk, r_i <= c_i), 1.0, 0.0).astype(BF16)
    g1, g2, g3 = _split3(g_raw)
    cum_f = _dot(tri_f, g1) + _dot(tri_f, g2) + _dot(tri_f, g3)
    cum_b = _dot(tri_b, g1) + _dot(tri_b, g2) + _dot(tri_b, g3)
    gc_s[0] = _expand_heads(cum_f, 2 * GDN_HEADS)
    gc_s[1] = _expand_heads(cum_b, 3 * GDN_HEADS)
    be_s[0] = _expand_heads(beta, 0)
    be_s[1] = _expand_heads(beta, GDN_HEADS)

    bdmask = bd_ref[...]
    eye = lvl_ref[N_LEVELS]
    ii = lax.broadcasted_iota(jnp.int32, (CHUNK, GROUP_W), 0)
    jj = lax.broadcasted_iota(jnp.int32, (CHUNK, GROUP_W), 1) % HEAD_DIM

    def setup(c, d, gi):
        rows = slice(c * CHUNK, (c + 1) * CHUNK)
        lanes = slice(gi * GROUP_W, (gi + 1) * GROUP_W)
        kk = k_s[rows, lanes]
        qq = q_s[rows, lanes]
        gc = gc_s[d, rows, lanes]
        be = be_s[d, rows, lanes]
        tri = (ii >= jj) if d == 0 else (ii <= jj)
        g_last = gc[CHUNK - 1:CHUNK, :] if d == 0 else gc[0:1, :]
        kb = kk * be
        e_gc = jnp.exp(gc)
        gc_t = jnp.sum(gc * eye, axis=0, keepdims=True)
        dec = jnp.exp(jnp.where(tri, gc - gc_t, MASK_NEG))
        qg_ref[d, 0, rows, lanes] = (qq * e_gc).astype(BF16)
        return dict(c=c, rows=rows, lanes=lanes, d=d, kk=kk, dec=dec, g_last=g_last,
                    lhs=jnp.concatenate([kb, qq], axis=0).astype(BF16),
                    vb=(v_s[rows, lanes] * be).astype(BF16), kbg=(kb * e_gc).astype(BF16),
                    kd=(kk * jnp.exp(g_last - gc)).astype(BF16))

    for c0 in range(0, CHUNKS_PER_TILE, LOCAL_PAR_CHUNKS):
        ps = [setup(c, d, gi) for c in range(c0, c0 + LOCAL_PAR_CHUNKS) for d in range(2)
              for gi in range(GDN_W // GROUP_W)]
        grams = [_dot_nt(p["lhs"], _block_diag(p["kk"].astype(BF16), bdmask)) for p in ps]
        kdts = [_dot_nt(eye.astype(BF16), _block_diag(p["kd"], bdmask)) for p in ps]
        ms = []
        for p, gram in zip(ps, grams):
            ms.append(gram[0:CHUNK] * p["dec"])
            qk_ref[p["d"], 0, p["rows"], p["lanes"]] = (gram[CHUNK:2 * CHUNK] * p["dec"]).astype(BF16)
        xs = [eye - m * lvl_ref[0] for m in ms]
        xb = [x.astype(BF16) for x in xs]
        mt = [jnp.concatenate([m.astype(BF16)] * HEADS_PER_GROUP, axis=0) for m in ms]
        for lv in range(1, N_LEVELS):
            ys = [_dot(x, m * lvlbd_ref[lv]) for x, m in zip(xb, mt)]
            zs = [_dot(y.astype(BF16), _block_diag(x, bdmask)) for y, x in zip(ys, xb)]
            xs = [x - z for x, z in zip(xs, zs)]
            xb = [x.astype(BF16) for x in xs]
        uws = [_dot(x, jnp.concatenate([_block_diag(p["vb"], bdmask), _block_diag(p["kbg"], bdmask)],
                                       axis=1)) for x, p in zip(xb, ps)]
        us = [uw[:, 0:GROUP_W] for uw in uws]
        ws = [uw[:, GROUP_W:2 * GROUP_W] for uw in uws]
        for p, u, w, kdt in zip(ps, us, ws, kdts):
            u_ref[p["d"], 0, p["rows"], p["lanes"]] = u.astype(BF16)
            w_ref[p["d"], 0, p["rows"], p["lanes"]] = w.astype(BF16)
            kdt_ref[p["d"], 0, p["rows"], p["lanes"]] = kdt.astype(BF16)
            eg_ref[p["d"], 0, 0, p["c"]:p["c"] + 1, p["lanes"]] = jnp.exp(p["g_last"])


def _gdn_local(gqkv, gates, conv_w, avec, dtvec, cos_t, sin_t, ones_bd, bdmask, lvl, lvlbd):
    nb, s_len, width = gqkv.shape
    ns = s_len // TILE
    hpt = TILE // HALO
    nh = s_len // HALO
    dir_spec = pl.BlockSpec((2, 1, TILE, GDN_W), lambda b, s: (0, b, s, 0))
    dir_shape = jax.ShapeDtypeStruct((2, nb, s_len, GDN_W), BF16)
    return pl.pallas_call(
        _gdn_local_kernel,
        grid=(nb, ns),
        in_specs=[pl.BlockSpec((1, TILE, width), lambda b, s: (b, s, 0)),
                  pl.BlockSpec((1, HALO, width), lambda b, s: (b, jnp.maximum(s * hpt - 1, 0), 0)),
                  pl.BlockSpec((1, HALO, width),
                               lambda b, s: (b, jnp.minimum((s + 1) * hpt, nh - 1), 0)),
                  pl.BlockSpec((1, TILE, GATE_PAD), lambda b, s: (b, s, 0)),
                  _const_spec(conv_w.shape),
                  _const_spec(avec.shape),
                  _const_spec(dtvec.shape),
                  pl.BlockSpec((TILE, 128), lambda b, s: (s, 0)),
                  pl.BlockSpec((TILE, 128), lambda b, s: (s, 0)),
                  _const_spec(ones_bd.shape),
                  _const_spec(bdmask.shape),
                  _const_spec(lvl.shape),
                  _const_spec(lvlbd.shape)],
        out_specs=[dir_spec] * 5 + [pl.BlockSpec((2, 1, 1, CHUNKS_PER_TILE, GDN_W),
                                                 lambda b, s: (0, b, s, 0, 0))],
        out_shape=[dir_shape] * 5 + [jax.ShapeDtypeStruct((2, nb, ns, CHUNKS_PER_TILE, GDN_W), F32)],
        scratch_shapes=[pltpu.VMEM((TILE, GDN_W), F32), pltpu.VMEM((TILE, GDN_W), F32),
                        pltpu.VMEM((TILE, GDN_W), F32), pltpu.VMEM((2, TILE, GDN_W), F32),
                        pltpu.VMEM((2, TILE, GDN_W), F32)],
        compiler_params=_cparams(2),
        name="gdn_local",
    )(gqkv, gqkv, gqkv, gates, conv_w, avec, dtvec, cos_t, sin_t, ones_bd, bdmask, lvl, lvlbd)


def _gdn_scan_kernel(uf, wf, qgf, qkf, kdf, egf, ub, wb, qgb, qkb, kdb, egb, bd_ref,
                     of_ref, ob_ref, state):
    j = pl.program_id(1)

    @pl.when(j == 0)
    def _():
        state[...] = jnp.zeros_like(state)

    bdmask = bd_ref[...]
    ins = ((uf, wf, qgf, qkf, kdf, egf, of_ref), (ub, wb, qgb, qkb, kdb, egb, ob_ref))
    n_groups = GDN_W // GROUP_W
    chains = [(bi, d, gi) for bi in range(SCAN_BATCH) for d in range(2) for gi in range(n_groups)]
    sts = [state[i] for i in range(len(chains))]
    for ci in range(CHUNKS_PER_TILE):
        sl = []
        for bi, d, gi in chains:
            c = ci if d == 0 else CHUNKS_PER_TILE - 1 - ci
            sl.append((c, slice(c * CHUNK, (c + 1) * CHUNK), slice(gi * GROUP_W, (gi + 1) * GROUP_W)))
        r1s = []
        for (bi, d, gi), (c, rows, lanes), st in zip(chains, sl, sts):
            w_r, qg_r = ins[d][1], ins[d][2]
            lhs1 = jnp.concatenate([w_r[0, bi, rows, lanes], qg_r[0, bi, rows, lanes]], axis=0)
            r1s.append(_dot(lhs1, _block_diag(st.astype(BF16), bdmask)))
        r2s = []
        for (bi, d, gi), (c, rows, lanes), r1 in zip(chains, sl, r1s):
            u_r, qk_r, kd_r = ins[d][0], ins[d][3], ins[d][4]
            v_new = u_r[0, bi, rows, lanes].astype(F32) - r1[0:CHUNK]
            lhs2 = jnp.concatenate([qk_r[0, bi, rows, lanes], kd_r[0, bi, rows, lanes]], axis=0)
            r2s.append(_dot(lhs2, _block_diag(v_new.astype(BF16), bdmask)))
        new_sts = []
        for (bi, d, gi), (c, rows, lanes), st, r1, r2 in zip(chains, sl, sts, r1s, r2s):
            eg_r, o_r = ins[d][5], ins[d][6]
            o_r[bi, rows, lanes] = (r1[CHUNK:2 * CHUNK] + r2[0:CHUNK]).astype(BF16)
            new_sts.append(st * eg_r[0, bi, 0, c:c + 1, lanes] + r2[CHUNK:2 * CHUNK])
        sts = new_sts
    for i, st in enumerate(sts):
        state[i] = st


def _gdn_scan(u, w, qg, qk, kdt, eg, bdmask):
    _, nb, s_len, _ = u.shape
    ns = s_len // TILE
    sb = SCAN_BATCH
    assert nb % sb == 0
    bwd = lambda j: jnp.where(j == 0, 0, ns - j)
    f_spec = pl.BlockSpec((1, sb, TILE, GDN_W), lambda b, j: (0, b, j, 0))
    b_spec = pl.BlockSpec((1, sb, TILE, GDN_W), lambda b, j: (1, b, bwd(j), 0))
    egf_spec = pl.BlockSpec((1, sb, 1, CHUNKS_PER_TILE, GDN_W), lambda b, j: (0, b, j, 0, 0))
    egb_spec = pl.BlockSpec((1, sb, 1, CHUNKS_PER_TILE, GDN_W), lambda b, j: (1, b, bwd(j), 0, 0))
    out_shape = jax.ShapeDtypeStruct((nb, s_len, GDN_W), BF16)
    return pl.pallas_call(
        _gdn_scan_kernel,
        grid=(nb // sb, ns),
        in_specs=[f_spec] * 5 + [egf_spec] + [b_spec] * 5 + [egb_spec] + [_const_spec(bdmask.shape)],
        out_specs=[pl.BlockSpec((sb, TILE, GDN_W), lambda b, j: (b, j, 0)),
                   pl.BlockSpec((sb, TILE, GDN_W), lambda b, j: (b, bwd(j), 0))],
        out_shape=[out_shape, out_shape],
        scratch_shapes=[pltpu.VMEM((sb * 2 * GDN_W // GROUP_W, CHUNK, GROUP_W), F32)],
        compiler_params=_cparams(2),
        name="gdn_scan",
    )(u, w, qg, qk, kdt, eg, u, w, qg, qk, kdt, eg, bdmask)


def _na_kernel(q_ref, k_ref, v_ref, bias_ref, o_ref, *, n_rows):
    s = pl.program_id(2)
    lane = lax.broadcasted_iota(jnp.int32, (1, 128), 1)
    head_mask = (lane < HEAD_DIM, lane >= HEAD_DIM)
    pair_lanes = [slice(128 * p, 128 * (p + 1)) for p in range(NA_PAIRS)]

    def stack_heads(q):
        return jnp.concatenate([jnp.where(head_mask[h], q, jnp.zeros_like(q)) for h in range(2)],
                               axis=0)

    def unstack_heads(o2, n):
        return jnp.where(head_mask[0], o2[0:n], o2[n:2 * n]).astype(BF16)

    @pl.when(s == 0)
    def _():
        q2s = [stack_heads(q_ref[0, :, ln]) for ln in pair_lanes]
        scs = [_dot_nt(q2, k_ref[0, 0:TILE, ln]) for q2, ln in zip(q2s, pair_lanes)]
        ps, dens = [], []
        for sc in scs:
            p = jnp.exp(sc - jnp.max(sc, axis=-1, keepdims=True))
            dens.append(jnp.sum(p, axis=-1, keepdims=True))
            ps.append(p.astype(BF16))
        os_ = [_dot(p, v_ref[0, 0:TILE, ln]) for p, ln in zip(ps, pair_lanes)]
        for o2, den, ln in zip(os_, dens, pair_lanes):
            o_ref[0, :, ln] = unstack_heads(o2 / den, TILE)

    @pl.when(s > 0)
    def _():
        rows_per_tile = TILE // GRID_W
        probs = []
        for p, ln in enumerate(pair_lanes):
            for r in range(rows_per_tile):
                row = (s - 1) * rows_per_tile + r
                rs = jnp.clip(row - NA_KH // 2, 0, n_rows - NA_KH)
                cls = rs - row + (NA_KH - 1)
                start = pl.multiple_of(TILE + rs * GRID_W, GRID_W)
                probs.append(dict(
                    p=p, r=r, ln=ln,
                    kw=k_ref[0, pl.ds(start, NA_KH * GRID_W), ln],
                    vw=v_ref[0, pl.ds(start, NA_KH * GRID_W), ln],
                    bias=jnp.concatenate([bias_ref[2 * p, cls], bias_ref[2 * p + 1, cls]], axis=0),
                    q2=stack_heads(q_ref[0, r * GRID_W:(r + 1) * GRID_W, ln])))
        sc_alls = [_dot_nt(jnp.concatenate([pb["q2"] for pb in probs if pb["p"] == p], axis=0),
                           k_ref[0, 0:TILE, ln]) for p, ln in enumerate(pair_lanes)]
        sws = [_dot_nt(pb["q2"], pb["kw"]) + pb["bias"] for pb in probs]
        pws, pcs, dens = [], [], []
        for pb, sw in zip(probs, sws):
            r = pb["r"]
            sc = sc_alls[pb["p"]][2 * GRID_W * r:2 * GRID_W * (r + 1)]
            mx = jnp.maximum(jnp.max(sw, axis=-1, keepdims=True), jnp.max(sc, axis=-1, keepdims=True))
            pw = jnp.exp(sw - mx)
            pc = jnp.exp(sc - mx)
            dens.append(jnp.sum(pw, axis=-1, keepdims=True) + jnp.sum(pc, axis=-1, keepdims=True))
            pws.append(pw.astype(BF16))
            pcs.append(pc.astype(BF16))
        oc_alls = [_dot(jnp.concatenate([pc for pb, pc in zip(probs, pcs) if pb["p"] == p], axis=0),
                        v_ref[0, 0:TILE, ln]) for p, ln in enumerate(pair_lanes)]
        ows = [_dot(pw, pb["vw"]) for pb, pw in zip(probs, pws)]
        for p, ln in enumerate(pair_lanes):
            outs = []
            for pb, ow, den in zip(probs, ows, dens):
                if pb["p"] == p:
                    r = pb["r"]
                    o2 = (ow + oc_alls[p][2 * GRID_W * r:2 * GRID_W * (r + 1)]) / den
                    outs.append(unstack_heads(o2, GRID_W))
            o_ref[0, :, ln] = jnp.concatenate(outs, axis=0)


def _na_attention(q, k, v, bias_tabs, layer):
    nb, s_len, _ = q.shape
    ns = s_len // TILE
    n_rows = (s_len - TILE) // GRID_W
    lanes = 128 * NA_PAIRS
    return pl.pallas_call(
        functools.partial(_na_kernel, n_rows=n_rows),
        grid=(nb, NA_W // lanes, ns),
        in_specs=[pl.BlockSpec((1, TILE, lanes), lambda b, hp, s: (b, s, hp)),
                  pl.BlockSpec((1, s_len, lanes), lambda b, hp, s: (b, 0, hp)),
                  pl.BlockSpec((1, s_len, lanes), lambda b, hp, s: (b, 0, hp)),
                  pl.BlockSpec((None, 2 * NA_PAIRS, NA_KH, GRID_W, NA_KH * GRID_W),
                               lambda b, hp, s: (layer, hp, 0, 0, 0))],
        out_specs=pl.BlockSpec((1, TILE, lanes), lambda b, hp, s: (b, s, hp)),
        out_shape=jax.ShapeDtypeStruct((nb, s_len, NA_W), BF16),
        compiler_params=_cparams(3),
        name="na_attn",
    )(q, k, v, bias_tabs)


def _na_bias_tables(rpb):
    n_l, n_h, n_dr, n_dc = rpb.shape
    period = 2 * GRID_W
    left = (GRID_W - 1) - (NA_KW - 1)
    e = jnp.pad(rpb.astype(F32), ((0, 0), (0, 0), (0, 0), (left, period - n_dc - left)))
    skew = jnp.tile(e, (1, 1, 1, GRID_W))[..., :GRID_W * (period - 1)]
    skew = skew.reshape(n_l, n_h, n_dr, GRID_W, period - 1)[..., GRID_W - 1:]
    qi = np.arange(GRID_W)[:, None]
    kc = np.arange(GRID_W)[None, :]
    cs = np.clip(qi - NA_KW // 2, 0, GRID_W - NA_KW)
    in_win = np.logical_and(kc >= cs, kc < cs + NA_KW)
    toe = jnp.where(in_win, skew, MASK_NEG)
    toe = jnp.transpose(toe, (0, 1, 3, 2, 4))
    tab = jnp.stack([toe[:, :, :, c:c + NA_KH, :] for c in range(NA_KH)], axis=2)
    return tab.reshape(n_l, n_h, NA_KH, GRID_W, NA_KH * GRID_W)


def _post_mixer_kernel(*refs):
    mod_ref, ona_ref, of_ref, ob_ref, z_ref, gg_ref, gp_ref, w_ref, ones_ref, o_ref = refs[-10:]
    x = _stream_tile(refs[:-10])
    d = x.shape[-1]
    gate = mod_ref[0][:, 2 * d:3 * d]
    ones_bd = ones_ref[...]
    blocks = [slice(i * POST_ROWS, (i + 1) * POST_ROWS) for i in range(TILE // POST_ROWS)]
    ogs, mss = [], []
    for rows in blocks:
        og = of_ref[0, rows, :].astype(F32) + ob_ref[0, rows, :].astype(F32)
        ogs.append(og)
        mss.append(_dot((og * og).astype(BF16), ones_bd) * (1.0 / HEAD_DIM))
    ys = []
    for rows, og, ms in zip(blocks, ogs, mss):
        gated = og * lax.rsqrt(ms + RMS_EPS) * gg_ref[...] * _silu(z_ref[0, rows, :].astype(F32))
        ys.append(_dot(ona_ref[0, rows, :], w_ref[0:NA_W, :])
                  + _dot(gated.astype(BF16), w_ref[NA_W:NA_W + GDN_W, :]))
    for rows, y in zip(blocks, ys):
        o_ref[0, rows, :] = x[rows, :] + gate * _rms(y, gp_ref[...])


def _post_mixer(stream, mod_all, o_na, o_f, o_b, z, gdn_gain, post_gain, w_out, ones_bd, layer):
    nb, s_len, _ = o_na.shape
    d = post_gain.shape[-1]
    x_specs, x_args = _stream_specs(stream, d)
    ns = s_len // TILE
    tok = lambda w: pl.BlockSpec((1, TILE, w), lambda b, s: (b, s, 0))
    return pl.pallas_call(
        _post_mixer_kernel,
        grid=(nb, ns),
        in_specs=x_specs + [_mod_spec(mod_all, layer, nb),
                            tok(NA_W), tok(GDN_W), tok(GDN_W), tok(GDN_W),
                            _const_spec(gdn_gain.shape), _const_spec(post_gain.shape),
                            _layer_spec(w_out, layer), _const_spec(ones_bd.shape)],
        out_specs=tok(d),
        out_shape=jax.ShapeDtypeStruct((nb, s_len, d), F32),
        input_output_aliases={} if isinstance(stream, tuple) else {0: 0},
        compiler_params=_cparams(2),
        name="post_mixer",
    )(*x_args, mod_all, o_na, o_f, o_b, z, gdn_gain, post_gain, w_out, ones_bd)


def _conv_ffn_kernel(x_ref, xp_ref, xn_ref, mod_ref, gpre_ref, gpost_ref, wu_ref, cw_ref, wd_ref,
                     o_ref, act_s, *, first_block):
    s = pl.program_id(1) + first_block
    ns = pl.num_programs(1) + first_block
    d = x_ref.shape[-1]
    d_ff = wd_ref.shape[0]
    m = mod_ref[0]
    shift, scale, gate = m[:, 3 * d:4 * d], m[:, 4 * d:5 * d], m[:, 5 * d:6 * d]
    x = x_ref[0]
    prev_ok = (s >= 2).astype(F32)
    next_ok = jnp.logical_and(s >= 1, s < ns - 1).astype(F32)

    def hidden(t):
        return _rms(t, gpre_ref[...]) * (1.0 + scale) + shift

    h = jnp.concatenate([hidden(xp_ref[0]) * prev_ok, hidden(x), hidden(xn_ref[0]) * next_ok],
                        axis=0).astype(BF16)
    n_steps = d_ff // FFN_COLS

    def up(c):
        return [_dot(h, wu_ref[:, off:off + FFN_COLS]) for off in (c * FFN_COLS, d_ff + c * FFN_COLS)]

    def activation(c, us):
        halves = []
        n_rows = TILE + 2 * HALO
        for off, u in zip((c * FFN_COLS, d_ff + c * FFN_COLS), us):
            u_m1 = pltpu.roll(u, 1, 0)[HALO:HALO + TILE]
            u_p1 = pltpu.roll(u, n_rows - 1, 0)[HALO:HALO + TILE]
            halves.append(cw_ref[0:1, off:off + FFN_COLS] * u_m1
                          + cw_ref[1:2, off:off + FFN_COLS] * u[HALO:HALO + TILE]
                          + cw_ref[2:3, off:off + FFN_COLS] * u_p1)
        return (_silu(halves[0]) * halves[1]).astype(BF16)

    u_next = up(0)
    for c in range(n_steps):
        u_cur = u_next
        if c + 1 < n_steps:
            u_next = up(c + 1)
        act_s[:, c * FFN_COLS:(c + 1) * FFN_COLS] = activation(c, u_cur)
    acc = _dot(act_s[...], wd_ref[...])
    o_ref[0] = x + gate * _rms(acc, gpost_ref[...])


def _conv_ffn(x_all, mod_all, pre_gain, post_gain, w_up, conv_w, w_down, layer, latent_only):
    nb, s_len, d = x_all.shape
    ns = s_len // TILE
    hpt = TILE // HALO
    nh = s_len // HALO
    first = 1 if latent_only else 0
    return pl.pallas_call(
        functools.partial(_conv_ffn_kernel, first_block=first),
        grid=(nb, ns - first),
        in_specs=[pl.BlockSpec((1, TILE, d), lambda b, s: (b, s + first, 0)),
                  pl.BlockSpec((1, HALO, d),
                               lambda b, s: (b, jnp.maximum((s + first) * hpt - 1, 0), 0)),
                  pl.BlockSpec((1, HALO, d),
                               lambda b, s: (b, jnp.minimum((s + first + 1) * hpt, nh - 1), 0)),
                  _mod_spec(mod_all, layer, nb, first),
                  _const_spec(pre_gain.shape), _const_spec(post_gain.shape),
                  _layer_spec(w_up, layer), _const_spec(conv_w.shape), _layer_spec(w_down, layer)],
        out_specs=pl.BlockSpec((1, TILE, d), lambda b, s: (b, s, 0)),
        out_shape=jax.ShapeDtypeStruct((nb, s_len - first * TILE, d), F32),
        scratch_shapes=[pltpu.VMEM((TILE, w_down.shape[1]), BF16)],
        compiler_params=_cparams(2),
        name="conv_ffn",
    )(x_all, x_all, x_all, mod_all, pre_gain, post_gain, w_up, conv_w, w_down)


def _rotary_tables(n_ctx, n_lat):
    t = jnp.arange(n_lat)
    row = (t // GRID_W).astype(F32)
    col = (t % GRID_W).astype(F32)
    pairs = HEAD_DIM // 4
    inv_freq = ROPE_BASE ** (-jnp.arange(pairs, dtype=F32) / pairs)
    ang = jnp.concatenate([row[:, None] * inv_freq, col[:, None] * inv_freq], axis=-1)
    cos, sin = jnp.cos(ang), jnp.sin(ang)
    cos64 = jnp.concatenate([cos, cos], axis=-1)
    sin64 = jnp.concatenate([-sin, sin], axis=-1)
    cos_t = jnp.concatenate([jnp.ones((n_ctx, HEAD_DIM), F32), cos64], axis=0)
    sin_t = jnp.concatenate([jnp.zeros((n_ctx, HEAD_DIM), F32), sin64], axis=0)
    return jnp.tile(cos_t, (1, 2)), jnp.tile(sin_t, (1, 2))


def _static_masks():
    lane_h = np.arange(GROUP_W) // HEAD_DIM
    bdmask = (lane_h[:, None] == lane_h[None, :]).astype(np.float32)
    seg = np.arange(GDN_W) // HEAD_DIM
    ones_bd = (seg[:, None] == seg[None, :]).astype(np.float32)
    i = np.arange(CHUNK)[:, None]
    j = (np.arange(GROUP_W) % HEAD_DIM)[None, :]
    lvl = []
    for lv in range(N_LEVELS):
        n = 2 ** lv
        lvl.append(np.logical_and(i // (2 * n) == j // (2 * n), i // n != j // n))
    lvl.append(i == j)
    lvl = np.stack(lvl).astype(np.float32)
    lvlbd = np.tile(lvl[:N_LEVELS], (1, HEADS_PER_GROUP, 1)) * bdmask[None]
    return (jnp.asarray(bdmask, BF16), jnp.asarray(ones_bd, BF16), jnp.asarray(lvl, F32),
            jnp.asarray(lvlbd, BF16))


def _gate_lane_vector(p):
    flat = p.reshape(1, 2 * GDN_HEADS).astype(F32)
    return jnp.pad(flat, ((0, 0), (2 * GDN_HEADS, GATE_PAD - 4 * GDN_HEADS)))


def kernel(x, c, ctx, c_ctx, ada_w, ada_b, norm_mix_pre, norm_mix_post, w_in, qkv_conv, a_log,
           dt_bias, gdn_norm, rpb, w_out, norm_ffn_pre, norm_ffn_post, ffn_up, ffn_conv, ffn_down):
    nb, n_lat, d = x.shape
    n_ctx = ctx.shape[1]
    depth = ada_w.shape[0]
    assert n_ctx == TILE and n_lat % TILE == 0 and n_lat // GRID_W >= NA_KH

    mod_rows = -(-(nb + 1) // 8) * 8
    cc = jnp.concatenate([c, c_ctx[None, :], jnp.zeros((mod_rows - nb - 1, d), F32)], axis=0)
    mod_all = _modulation(cc, ada_w, ada_b).reshape(depth, mod_rows, 1, N_MOD * d)

    cos_t, sin_t = _rotary_tables(n_ctx, n_lat)
    bdmask, ones_bd, lvl, lvlbd = _static_masks()
    in_w = w_in.shape[-1]
    w_in_p = jnp.pad(w_in, ((0, 0), (0, 0), (0, GATE_PAD - (in_w - 3 * NA_W - 4 * GDN_W)))).astype(BF16)
    w_out_b = w_out.astype(BF16)
    w_up_b = ffn_up.astype(BF16)
    w_down_b = ffn_down.astype(BF16)
    bias_tabs = _na_bias_tables(rpb)

    s_len = n_ctx + n_lat
    stream = (ctx, x)
    for l in range(depth):
        naq, nak, nav, gqkv, z, gates = _pre_mixer(stream, s_len, mod_all, norm_mix_pre[l][None, :],
                                                   w_in_p, l)
        u, w, qg, qk, kdt, eg = _gdn_local(gqkv, gates, qkv_conv[l], _gate_lane_vector(a_log[l]),
                                           _gate_lane_vector(dt_bias[l]), cos_t, sin_t, ones_bd,
                                           bdmask, lvl, lvlbd)
        o_f, o_b = _gdn_scan(u, w, qg, qk, kdt, eg, bdmask)
        o_na = _na_attention(naq, nak, nav, bias_tabs, l)
        stream = _post_mixer(stream, mod_all, o_na, o_f, o_b, z, jnp.tile(gdn_norm[l], GDN_HEADS)[None, :],
                             norm_mix_post[l][None, :], w_out_b, ones_bd, l)
        stream = _conv_ffn(stream, mod_all, norm_ffn_pre[l][None, :], norm_ffn_post[l][None, :],
                           w_up_b, ffn_conv[l], w_down_b, l, latent_only=(l == depth - 1))
    return stream
```
